```python
import math
import jax, jax.numpy as jnp
from jax import lax
import numpy as np

D_MODEL = 1024
BATCH = 8
SEQ = 4096
DEPTH = 4

GRID_W = 64
CTX_LEN = 256
EPS = 1e-6
CONV_WIDTH = 4
GDN_HEAD_DIM = 128
GDN_HEADS = (D_MODEL // 2) // GDN_HEAD_DIM
GDN_WIDTH = GDN_HEADS * GDN_HEAD_DIM
GDN_CHUNK = 64
LRU_WIDTH = D_MODEL // 4
LRU_BLOCKS = 4
LRU_BLOCK = LRU_WIDTH // LRU_BLOCKS
LRU_C = 8.0
MLA_V = 64
MLA_HEADS = (D_MODEL // 4) // MLA_V
MLA_WIDTH = MLA_HEADS * MLA_V
MLA_NOPE = 64
MLA_ROPE = 32
MLA_Q_RANK = D_MODEL // 4
MLA_KV_RANK = D_MODEL // 8
ROPE_BASE = 10000.0
Q_BLOCK = 128
D_FF = 4 * D_MODEL
MIX_WIDTH = GDN_WIDTH + LRU_WIDTH + MLA_WIDTH
IN_SIZES = (3 * GDN_WIDTH, GDN_WIDTH, 2 * GDN_HEADS, 2 * GDN_HEADS, LRU_WIDTH, LRU_WIDTH, MLA_Q_RANK, MLA_KV_RANK, MLA_ROPE)
IN_WIDTH = sum(IN_SIZES)

kernel_name = 'hybrid_gdn_rglru_mla_dit_block'


def rms_norm(x, g):
    xf = x.astype(jnp.float32)
    y = xf * lax.rsqrt(jnp.mean(xf * xf, axis=-1, keepdims=True) + EPS)
    return (y * g.astype(jnp.float32)).astype(x.dtype)


def l2norm(t):
    return t * lax.rsqrt(jnp.sum(t * t, axis=-1, keepdims=True) + EPS)


def split_cols(t, sizes):
    idx = np.cumsum(sizes)[:-1].tolist()
    return jnp.split(t, idx, axis=-1)


def adaln(cond, w, b):
    return jnp.split(jax.nn.silu(cond) @ w + b, 6, axis=-1)


def dw_conv(x, w):
    ch = x.shape[-1]
    left = CONV_WIDTH // 2
    return lax.conv_general_dilated(x, w[:, None, :].astype(x.dtype), window_strides=(1,),
                                    padding=[(left, CONV_WIDTH - 1 - left)],
                                    dimension_numbers=('NWC', 'WIO', 'NWC'), feature_group_count=ch)


def axial_rope_tables(rows):
    row = jnp.repeat(jnp.arange(rows, dtype=jnp.float32), GRID_W)
    col = jnp.tile(jnp.arange(GRID_W, dtype=jnp.float32), rows)
    half = MLA_ROPE // 2
    inv = ROPE_BASE ** (-jnp.arange(0, half, 2, dtype=jnp.float32) / half)
    ang = jnp.stack([row[:, None] * inv, col[:, None] * inv], axis=1)
    ang = jnp.concatenate([ang, ang], axis=-1)[:, None]
    return jnp.cos(ang), jnp.sin(ang)


def apply_axial_rope(t, rope):
    cos, sin = rope
    shp = t.shape
    tr = t.astype(jnp.float32).reshape(shp[:-1] + (2, shp[-1] // 2))
    t1, t2 = jnp.split(tr, 2, axis=-1)
    rot = jnp.concatenate([-t2, t1], axis=-1)
    return (tr * cos + rot * sin).reshape(shp).astype(t.dtype)


def gdn_chunked(q, k, v, g, beta, s0):
    B, L, H, dk = q.shape
    dv = v.shape[-1]
    n = L // GDN_CHUNK

    def chunks(t):
        t = t.reshape((B, n, GDN_CHUNK, H) + t.shape[3:])
        return jnp.moveaxis(t, (1, 3), (0, 2))

    q = chunks(q) * (dk ** -0.5)
    k, v, g, beta = chunks(k), chunks(v), chunks(g), chunks(beta)
    G = jnp.cumsum(g, axis=-1)
    idx = jnp.arange(GDN_CHUNK)
    incl = idx[:, None] >= idx[None, :]
    strict = idx[:, None] > idx[None, :]
    decay = jnp.exp(jnp.where(incl, G[..., :, None] - G[..., None, :], -jnp.inf))
    kb = k * beta[..., None]
    low = jnp.where(strict, jnp.einsum('nbhid,nbhjd->nbhij', kb, k) * decay, 0.0)
    amat = low + jnp.eye(GDN_CHUNK, dtype=low.dtype)
    rhs = jnp.concatenate([v * beta[..., None], kb * jnp.exp(G)[..., None]], axis=-1)
    sol = lax.linalg.triangular_solve(amat, rhs, left_side=True, lower=True, unit_diagonal=True)
    u, w = sol[..., :dv], sol[..., dv:]
    attn = jnp.einsum('nbhid,nbhjd->nbhij', q, k) * decay
    qg = q * jnp.exp(G)[..., None]
    kg = k * jnp.exp(G[..., -1:] - G)[..., None]
    glast = jnp.exp(G[..., -1])

    def step(S, inp):
        u_i, w_i, a_i, qg_i, kg_i, gl_i = inp
        v_new = u_i - jnp.einsum('bhcd,bhde->bhce', w_i, S)
        o_i = jnp.einsum('bhcd,bhde->bhce', qg_i, S) + jnp.einsum('bhij,bhje->bhie', a_i, v_new)
        S = S * gl_i[..., None, None] + jnp.einsum('bhcd,bhce->bhde', kg_i, v_new)
        return S, o_i

    S, o = lax.scan(step, s0, (u, w, attn, qg, kg, glast))
    o = jnp.moveaxis(o, (0, 2), (1, 3)).reshape(B, L, H, dv)
    return o, S


def gdn_seq(qkv, b_col, a_col, conv_w, a_log, dt_bias, s0_f, s0_b):
    B, L, _ = qkv.shape
    qkv = jax.nn.silu(dw_conv(qkv, conv_w)).astype(jnp.float32)
    q, k, v = [t.reshape(B, L, GDN_HEADS, GDN_HEAD_DIM) for t in jnp.split(qkv, 3, axis=-1)]
    q, k = l2norm(q), l2norm(k)
    beta = jax.nn.sigmoid(b_col.astype(jnp.float32)).reshape(B, L, 2, GDN_HEADS)
    g = -jnp.exp(a_log.astype(jnp.float32)) * jax.nn.softplus(
        a_col.astype(jnp.float32).reshape(B, L, 2, GDN_HEADS) + dt_bias.astype(jnp.float32))
    o_f, s_f = gdn_chunked(q, k, v, g[:, :, 0], beta[:, :, 0], s0_f)
    rev = lambda t: jnp.flip(t, axis=1)
    o_b, s_b = gdn_chunked(rev(q), rev(k), rev(v), rev(g[:, :, 1]), rev(beta[:, :, 1]), s0_b)
    return o_f + rev(o_b), s_f, s_b


def gdn_out(o, z, norm_w):
    B, L = z.shape[:2]
    zf = z.astype(jnp.float32).reshape(B, L, GDN_HEADS, GDN_HEAD_DIM)
    return (rms_norm(o, norm_w) * jax.nn.silu(zf)).reshape(B, L, GDN_WIDTH)


def rg_lru_coeffs(xc, w_a, b_a, w_i, b_i, lam):
    B, L, W = xc.shape
    xb = xc.reshape(B, L, LRU_BLOCKS, LRU_BLOCK)
    r = jax.nn.sigmoid(jnp.einsum('blgi,gij->blgj', xb, w_a).reshape(B, L, W) + b_a)
    i = jax.nn.sigmoid(jnp.einsum('blgi,gij->blgj', xb, w_i).reshape(B, L, W) + b_i)
    log_a = -LRU_C * r * jax.nn.softplus(-lam)
    a = jnp.exp(log_a)
    return a, jnp.sqrt(-jnp.expm1(2.0 * log_a)) * (i * xc)


def linear_scan(a, b, h0):
    b = b.at[:, 0].add(a[:, 0] * h0)

    def combine(e1, e2):
        return e1[0] * e2[0], e2[0] * e1[1] + e2[1]

    return lax.associative_scan(combine, (a, b), axis=1)[1]


def lru_seq(xr, conv_w, conv_b, w_a, b_a, w_i, b_i, lam, h0_f, h0_b):
    xc = (dw_conv(xr, conv_w) + conv_b).astype(jnp.float32)
    a_f, u_f = rg_lru_coeffs(xc, w_a[0], b_a[0], w_i[0], b_i[0], lam[0])
    h_f = linear_scan(a_f, u_f, h0_f)
    a_b, u_b = rg_lru_coeffs(jnp.flip(xc, axis=1), w_a[1], b_a[1], w_i[1], b_i[1], lam[1])
    h_b = linear_scan(a_b, u_b, h0_b)
    return h_f + jnp.flip(h_b, axis=1), h_f[:, -1], h_b[:, -1]


def mla_q(cq, q_norm_w, w_uq, rope):
    B, L, _ = cq.shape
    q = (rms_norm(cq, q_norm_w) @ w_uq).reshape(B, L, MLA_HEADS, MLA_NOPE + MLA_ROPE)
    q_rope = q[..., MLA_NOPE:]
    if rope is not None:
        q_rope = apply_axial_rope(q_rope, rope)
    return jnp.concatenate([q[..., :MLA_NOPE], q_rope], axis=-1)


def mla_kv(ckv, kr, kv_norm_w, w_ukv, rope):
    B, L, _ = ckv.shape
    kv = (rms_norm(ckv, kv_norm_w) @ w_ukv).reshape(B, L, MLA_HEADS, MLA_NOPE + MLA_V)
    kr = kr[:, :, None, :]
    if rope is not None:
        kr = apply_axial_rope(kr, rope)
    k = jnp.concatenate([kv[..., :MLA_NOPE], jnp.broadcast_to(kr, (B, L, MLA_HEADS, MLA_ROPE))], axis=-1)
    return k, kv[..., MLA_NOPE:]


def softmax_attend(q, k, v):
    s = jnp.einsum('bqhd,bkhd->bhqk', q, k).astype(jnp.float32) * ((MLA_NOPE + MLA_ROPE) ** -0.5)
    p = jax.nn.softmax(s, axis=-1)
    return jnp.einsum('bhqk,bkhd->bqhd', p.astype(v.dtype), v)


def blocked_attend(q, k, v):
    B, L, H, d = q.shape
    nb = L // Q_BLOCK
    qb = jnp.moveaxis(q.reshape(B, nb, Q_BLOCK, H, d), 1, 0)
    out = lax.map(lambda qi: softmax_attend(qi, k, v), qb)
    return jnp.moveaxis(out, 0, 1).reshape(B, L, H, v.shape[-1])


def hybrid_mixer(h_lat, h_ctx, with_ctx_out, w_in, gdn_conv_w, gdn_a_log, gdn_dt_bias, gdn_norm_w,
                 lru_conv_w, lru_conv_b, lru_w_a, lru_b_a, lru_w_i, lru_b_i, lru_lambda,
                 mla_q_norm, mla_w_uq, mla_kv_norm, mla_w_ukv, w_out, rope):
    B, L, _ = h_lat.shape
    pc = split_cols(h_ctx @ w_in, IN_SIZES)
    pl = split_cols(h_lat @ w_in, IN_SIZES)
    s_zero = jnp.zeros((B, GDN_HEADS, GDN_HEAD_DIM, GDN_HEAD_DIM), jnp.float32)
    oc_gdn, s_f, s_b = gdn_seq(pc[0], pc[2], pc[3], gdn_conv_w, gdn_a_log, gdn_dt_bias, s_zero, s_zero)
    ol_gdn, _, _ = gdn_seq(pl[0], pl[2], pl[3], gdn_conv_w, gdn_a_log, gdn_dt_bias, s_f, s_b)
    h_zero = jnp.zeros((B, LRU_WIDTH), jnp.float32)
    rc, hf, hb = lru_seq(pc[4], lru_conv_w, lru_conv_b, lru_w_a, lru_b_a, lru_w_i, lru_b_i, lru_lambda, h_zero, h_zero)
    rl, _, _ = lru_seq(pl[4], lru_conv_w, lru_conv_b, lru_w_a, lru_b_a, lru_w_i, lru_b_i, lru_lambda, hf, hb)
    k_c, v_c = mla_kv(pc[7], pc[8], mla_kv_norm, mla_w_ukv, None)
    k_l, v_l = mla_kv(pl[7], pl[8], mla_kv_norm, mla_w_ukv, rope)
    q_l = mla_q(pl[6], mla_q_norm, mla_w_uq, rope)
    a_l = blocked_attend(q_l, jnp.concatenate([k_c, k_l], axis=1), jnp.concatenate([v_c, v_l], axis=1))
    y_lat = jnp.concatenate([gdn_out(ol_gdn, pl[1], gdn_norm_w),
                             rl * jax.nn.gelu(pl[5].astype(jnp.float32)),
                             a_l.reshape(B, L, MLA_WIDTH).astype(jnp.float32)], axis=-1).astype(h_lat.dtype) @ w_out
    if not with_ctx_out:
        return y_lat, None
    Lc = h_ctx.shape[1]
    a_c = softmax_attend(mla_q(pc[6], mla_q_norm, mla_w_uq, None), k_c, v_c)
    y_ctx = jnp.concatenate([gdn_out(oc_gdn, pc[1], gdn_norm_w),
                             rc * jax.nn.gelu(pc[5].astype(jnp.float32)),
                             a_c.reshape(B, Lc, MLA_WIDTH).astype(jnp.float32)], axis=-1).astype(h_ctx.dtype) @ w_out
    return y_lat, y_ctx


def sq_relu_mlp(h, w1, w2):
    return jnp.square(jax.nn.relu(h @ w1)) @ w2


def setup_inputs(seed: int = 0) -> dict:
    key = jax.random.key(seed)
    kit = iter(list(jax.random.split(key, 40)))
    nrm = lambda shape, scale: scale * jax.random.normal(next(kit), shape, jnp.float32)
    gain = lambda shape: 1.0 + 0.05 * jax.random.normal(next(kit), shape, jnp.float32)
    Ld = DEPTH
    u = jax.random.uniform(next(kit), (Ld, 2, LRU_WIDTH), jnp.float32, minval=0.9, maxval=0.999)
    a_base = u ** (1.0 / LRU_C)
    dt = jnp.exp(jax.random.uniform(next(kit), (Ld, 2, GDN_HEADS), jnp.float32,
                                    minval=math.log(1e-3), maxval=math.log(1e-1)))
    return {
        'x': nrm((BATCH, SEQ, D_MODEL), 1.0),
        'c': nrm((BATCH, D_MODEL), 1.0),
        'ctx': nrm((BATCH, CTX_LEN, D_MODEL), 1.0),
        'c_ctx': nrm((D_MODEL,), 1.0),
        'w_ada': nrm((Ld, D_MODEL, 6 * D_MODEL), 0.5 * D_MODEL ** -0.5),
        'b_ada': nrm((Ld, 6 * D_MODEL), 0.01),
        'g_attn_pre': gain((Ld, D_MODEL)),
        'g_attn_post': gain((Ld, D_MODEL)),
        'g_mlp_pre': gain((Ld, D_MODEL)),
        'g_mlp_post': gain((Ld, D_MODEL)),
        'w_in': nrm((Ld, D_MODEL, IN_WIDTH), D_MODEL ** -0.5),
        'gdn_conv_w': nrm((Ld, CONV_WIDTH, 3 * GDN_WIDTH), CONV_WIDTH ** -0.5),
        'gdn_a_log': jnp.log(jax.random.uniform(next(kit), (Ld, 2, GDN_HEADS), jnp.float32, minval=1.0, maxval=16.0)),
        'gdn_dt_bias': dt + jnp.log(-jnp.expm1(-dt)),
        'gdn_norm_w': gain((Ld, GDN_HEAD_DIM)),
        'lru_conv_w': nrm((Ld, CONV_WIDTH, LRU_WIDTH), CONV_WIDTH ** -0.5),
        'lru_conv_b': nrm((Ld, LRU_WIDTH), 0.01),
        'lru_w_a': nrm((Ld, 2, LRU_BLOCKS, LRU_BLOCK, LRU_BLOCK), LRU_BLOCK ** -0.5),
        'lru_b_a': nrm((Ld, 2, LRU_WIDTH), 0.01),
        'lru_w_i': nrm((Ld, 2, LRU_BLOCKS, LRU_BLOCK, LRU_BLOCK), LRU_BLOCK ** -0.5),
        'lru_b_i': nrm((Ld, 2, LRU_WIDTH), 0.01),
        'lru_lambda': jnp.log(a_base) - jnp.log1p(-a_base),
        'mla_q_norm': gain((Ld, MLA_Q_RANK)),
        'mla_w_uq': nrm((Ld, MLA_Q_RANK, MLA_HEADS * (MLA_NOPE + MLA_ROPE)), MLA_Q_RANK ** -0.5),
        'mla_kv_norm': gain((Ld, MLA_KV_RANK)),
        'mla_w_ukv': nrm((Ld, MLA_KV_RANK, MLA_HEADS * (MLA_NOPE + MLA_V)), MLA_KV_RANK ** -0.5),
        'w_out': nrm((Ld, MIX_WIDTH, D_MODEL), MIX_WIDTH ** -0.5),
        'w_mlp1': nrm((Ld, D_MODEL, D_FF), D_MODEL ** -0.5),
        'w_mlp2': nrm((Ld, D_FF, D_MODEL), D_FF ** -0.5),
    }


def reference(x, c, ctx, c_ctx, w_ada, b_ada, g_attn_pre, g_attn_post, g_mlp_pre, g_mlp_post, w_in,
              gdn_conv_w, gdn_a_log, gdn_dt_bias, gdn_norm_w, lru_conv_w, lru_conv_b, lru_w_a, lru_b_a,
              lru_w_i, lru_b_i, lru_lambda, mla_q_norm, mla_w_uq, mla_kv_norm, mla_w_ukv, w_out,
              w_mlp1, w_mlp2):
    n_lat = x.shape[1]
    ROWS = n_lat // GRID_W
    rope = axial_rope_tables(ROWS)
    for l in range(DEPTH):
        last = l == DEPTH - 1
        sh1, sc1, gt1, sh2, sc2, gt2 = [m[:, None, :] for m in adaln(c, w_ada[l], b_ada[l])]
        csh1, csc1, cgt1, csh2, csc2, cgt2 = adaln(c_ctx, w_ada[l], b_ada[l])
        h_lat = rms_norm(x, g_attn_pre[l]) * (1.0 + sc1) + sh1
        h_ctx = rms_norm(ctx, g_attn_pre[l]) * (1.0 + csc1) + csh1
        y_lat, y_ctx = hybrid_mixer(h_lat, h_ctx, not last, w_in[l], gdn_conv_w[l], gdn_a_log[l], gdn_dt_bias[l],
                                    gdn_norm_w[l], lru_conv_w[l], lru_conv_b[l], lru_w_a[l], lru_b_a[l],
                                    lru_w_i[l], lru_b_i[l], lru_lambda[l], mla_q_norm[l], mla_w_uq[l],
                                    mla_kv_norm[l], mla_w_ukv[l], w_out[l], rope)
        x = x + gt1 * rms_norm(y_lat, g_attn_post[l])
        h = rms_norm(x, g_mlp_pre[l]) * (1.0 + sc2) + sh2
        x = x + gt2 * rms_norm(sq_relu_mlp(h, w_mlp1[l], w_mlp2[l]), g_mlp_post[l])
        if not last:
            ctx = ctx + cgt1 * rms_norm(y_ctx, g_attn_post[l])
            hc = rms_norm(ctx, g_mlp_pre[l]) * (1.0 + csc2) + csh2
            ctx = ctx + cgt2 * rms_norm(sq_relu_mlp(hc, w_mlp1[l], w_mlp2[l]), g_mlp_post[l])
    return x
```

```python
import functools

import numpy as np
import jax
import jax.numpy as jnp
from jax import lax
from jax.experimental import pallas as pl
from jax.experimental.pallas import tpu as pltpu

F32 = jnp.float32
BF16 = jnp.bfloat16
HIGHEST = lax.Precision.HIGHEST

D_MODEL = 1024
EPS = 1e-6
GRID_W = 64
CONV_WIDTH = 4
GDN_HEAD_DIM = 128
GDN_HEADS = 4
GDN_WIDTH = GDN_HEADS * GDN_HEAD_DIM
GDN_CHUNK = 64
LRU_WIDTH = 256
LRU_BLOCKS = 4
LRU_C = 8.0
MLA_V = 64
MLA_HEADS = 4
MLA_WIDTH = MLA_HEADS * MLA_V
MLA_NOPE = 64
MLA_ROPE = 32
MLA_Q_RANK = 256
MLA_KV_RANK = 128
ROPE_BASE = 10000.0
D_FF = 4 * D_MODEL
IN_SIZES = (3 * GDN_WIDTH, GDN_WIDTH, 2 * GDN_HEADS, 2 * GDN_HEADS, LRU_WIDTH, LRU_WIDTH,
            MLA_Q_RANK, MLA_KV_RANK, MLA_ROPE)

LANES = 128
SUBLANES = 8
TM = 512
MAIN_COLS = 3 * GDN_WIDTH + GDN_WIDTH + 2 * LRU_WIDTH
REST_COLS = MLA_Q_RANK + MLA_KV_RANK + 3 * LANES
CONV_ROWS = 256
ATT_TQ = 256
ATT_KB = 256
FF_CHUNK = 1024
VMEM_LIMIT = 56 * 1024 * 1024

_NT = (((1,), (1,)), ((), ()))
_TN = (((0,), (0,)), ((), ()))


def _rms(x, g):
    return x * lax.rsqrt(jnp.mean(x * x, axis=-1, keepdims=True) + EPS) * g


def _bdot(a, b):
    return jnp.dot(a.astype(BF16), b.astype(BF16), preferred_element_type=F32)


def _params(*sem):
    return pltpu.CompilerParams(dimension_semantics=sem, vmem_limit_bytes=VMEM_LIMIT)


def _ada_kernel(c_ref, w_ref, b_ref, o_ref):
    o_ref[...] = _bdot(jax.nn.silu(c_ref[...]), w_ref[...]) + b_ref[...]


def _ada_call(cond, w_ada, b_ada):
    depth = w_ada.shape[0]
    rows = cond.shape[0]
    return pl.pallas_call(
        _ada_kernel,
        out_shape=jax.ShapeDtypeStruct((depth, rows, 6 * D_MODEL), F32),
        grid=(depth, 6),
        in_specs=[pl.BlockSpec((rows, D_MODEL), lambda l, j: (0, 0)),
                  pl.BlockSpec((None, D_MODEL, D_MODEL), lambda l, j: (l, 0, j)),
                  pl.BlockSpec((None, 1, D_MODEL), lambda l, j: (l, 0, j))],
        out_specs=pl.BlockSpec((None, rows, D_MODEL), lambda l, j: (l, 0, j)),
        compiler_params=_params("arbitrary", "arbitrary"),
        name="adaln",
    )(cond, w_ada, b_ada.reshape(depth, 1, 6 * D_MODEL))


def _inproj_kernel(x_ref, mod_ref, g_ref, wm_ref, wr_ref, qn_ref, wq_ref, kvn_ref, wkn_ref, wv_ref, tab_ref,
                   p_ref, bg_ref, q_ref, k_ref, v_ref):
    h = _rms(x_ref[...], g_ref[...]) * (1.0 + mod_ref[1:2, :]) + mod_ref[0:1, :]
    hb = h.astype(BF16)
    p_ref[...] = jnp.dot(hb, wm_ref[...], preferred_element_type=F32)
    r = jnp.dot(hb, wr_ref[...], preferred_element_type=F32)
    o0 = MLA_Q_RANK
    o1 = o0 + MLA_KV_RANK
    cq = r[:, :o0]
    ckv = r[:, o0:o1]
    bg_ref[...] = r[:, o1:o1 + LANES]
    kr_a = r[:, o1 + LANES:o1 + 2 * LANES]
    kr_b = r[:, o1 + 2 * LANES:o1 + 3 * LANES]
    tab = tab_ref[...]
    tile = lambda t: jnp.concatenate([t] * MLA_HEADS, axis=1)
    q2 = _bdot(_rms(cq, qn_ref[...]), wq_ref[...])
    hw = MLA_HEADS * LANES
    q_ref[...] = (q2[:, :hw] * tile(tab[:, :LANES]) + q2[:, hw:] * tile(tab[:, LANES:2 * LANES])).astype(BF16)
    ckvn = _rms(ckv, kvn_ref[...]).astype(BF16)
    kr = kr_a * tab[:, 2 * LANES:3 * LANES] + kr_b * tab[:, 3 * LANES:]
    k_ref[...] = (jnp.dot(ckvn, wkn_ref[...], preferred_element_type=F32) + tile(kr)).astype(BF16)
    v_ref[...] = jnp.dot(ckvn, wv_ref[...], preferred_element_type=F32).astype(BF16)


def _inproj_call(xs, mods_l, g, wm, wr, qn, wq, kvn, wkn, wv, tab, *, batch, seq):
    n = xs.shape[0]
    nblk = n // TM
    nlb = batch * seq // TM
    per_b = seq // TM
    modrow = lambda i: jnp.where(i < nlb, i // per_b, batch)
    tabrow = lambda i: jnp.where(i < nlb, i % per_b, per_b)
    full = lambda a: pl.BlockSpec(a.shape, lambda i: (0,) * a.ndim)
    rowblk = lambda w: pl.BlockSpec((TM, w), lambda i: (i, 0))
    hw = MLA_HEADS * LANES
    return pl.pallas_call(
        _inproj_kernel,
        out_shape=(jax.ShapeDtypeStruct((n, MAIN_COLS), F32),
                   jax.ShapeDtypeStruct((n, LANES), F32),
                   jax.ShapeDtypeStruct((n, hw), BF16),
                   jax.ShapeDtypeStruct((n, hw), BF16),
                   jax.ShapeDtypeStruct((n, MLA_WIDTH), BF16)),
        grid=(nblk,),
        in_specs=[rowblk(D_MODEL),
                  pl.BlockSpec((None, 6, D_MODEL), lambda i: (modrow(i), 0, 0)),
                  full(g), full(wm), full(wr), full(qn), full(wq), full(kvn), full(wkn), full(wv),
                  pl.BlockSpec((TM, 4 * LANES), lambda i: (tabrow(i), 0))],
        out_specs=(rowblk(MAIN_COLS), rowblk(LANES), rowblk(hw), rowblk(hw), rowblk(MLA_WIDTH)),
        compiler_params=_params("arbitrary"),
        name="inproj",
    )(xs, mods_l, g, wm, wr, qn, wq, kvn, wkn, wv, tab)


def _conv_chunk(src_ref, r, n_chunks, n_rows, w):
    start = pl.multiple_of(r * CONV_ROWS, CONV_ROWS)
    cur = src_ref[pl.ds(start, CONV_ROWS), :]
    prev_start = pl.multiple_of(jnp.maximum(start - SUBLANES, 0), SUBLANES)
    prev = jnp.where(r > 0, src_ref[pl.ds(prev_start, SUBLANES), :], 0.0)
    next_start = pl.multiple_of(jnp.minimum(start + CONV_ROWS, n_rows - SUBLANES), SUBLANES)
    nxt = jnp.where(r < n_chunks - 1, src_ref[pl.ds(next_start, SUBLANES), :], 0.0)
    cat = jnp.concatenate([prev, cur, nxt], axis=0)
    tot = CONV_ROWS + 2 * SUBLANES
    sl = slice(SUBLANES, SUBLANES + CONV_ROWS)
    acc = w[2:3, :] * cur
    acc = acc + w[0:1, :] * pltpu.roll(cat, 2, axis=0)[sl]
    acc = acc + w[1:2, :] * pltpu.roll(cat, 1, axis=0)[sl]
    acc = acc + w[3:4, :] * pltpu.roll(cat, tot - 1, axis=0)[sl]
    return start, acc


def _gdn_kernel(ql_ref, kl_ref, vl_ref, qc_ref, kc_ref, vc_ref, zl_ref, zc_ref, bgl_ref, bgc_ref,
                cwq_ref, cwk_ref, cwv_ref, ald_ref, nw_ref, yl_ref, yc_ref,
                qs, ks, vs, gs, oacc, *, seq, ctx):
    head = pl.program_id(1)
    C = GDN_CHUNK
    hd = GDN_HEAD_DIM

    def prep(src_ref, dst, off, n_rows, cw_ref, kind):
        n_chunks = n_rows // CONV_ROWS
        w = cw_ref[...]

        def body(r, carry):
            start, acc = _conv_chunk(src_ref, r, n_chunks, n_rows, w)
            y = jax.nn.silu(acc)
            if kind != "v":
                y = y * lax.rsqrt(jnp.sum(y * y, axis=-1, keepdims=True) + EPS)
            if kind == "q":
                y = y * (hd ** -0.5)
            dst[pl.ds(off + start, CONV_ROWS), :] = y
            return carry

        lax.fori_loop(0, n_chunks, body, 0)

    def prep_gates(src_ref, off, n_rows):
        lane = lax.broadcasted_iota(jnp.int32, (CONV_ROWS, LANES), 1)
        a_log = ald_ref[0:1, :]
        dt_b = ald_ref[1:2, :]

        def body(r, carry):
            start = pl.multiple_of(r * CONV_ROWS, CONV_ROWS)
            bg = src_ref[pl.ds(start, CONV_ROWS), :]
            beta = jax.nn.sigmoid(bg)
            g = -jnp.exp(a_log) * jax.nn.softplus(bg + dt_b)
            gs[pl.ds(off + start, CONV_ROWS), :] = jnp.where(lane < 2 * GDN_HEADS, beta, g)
            oacc[pl.ds(off + start, CONV_ROWS), :] = jnp.zeros((CONV_ROWS, LANES), F32)
            return carry

        lax.fori_loop(0, n_rows // CONV_ROWS, body, 0)

    for src_c, src_l, dst, cw, kind in ((qc_ref, ql_ref, qs, cwq_ref, "q"), (kc_ref, kl_ref, ks, cwk_ref, "k"),
                                        (vc_ref, vl_ref, vs, cwv_ref, "v")):
        prep(src_c, dst, 0, ctx, cw, kind)
        prep(src_l, dst, ctx, seq, cw, kind)
    prep_gates(bgc_ref, 0, ctx)
    prep_gates(bgl_ref, ctx, seq)

    n_ctx = ctx // C
    n_tot = (ctx + seq) // C
    lane = lax.broadcasted_iota(jnp.int32, (C, LANES), 1)
    sub_t = lax.broadcasted_iota(jnp.int32, (LANES, C), 0)
    row = lax.broadcasted_iota(jnp.int32, (C, C), 0)
    col = lax.broadcasted_iota(jnp.int32, (C, C), 1)
    eye = (row == col).astype(F32)

    def chunk(c, S, fwd):
        r0 = pl.multiple_of(c * C, C)
        q = qs[pl.ds(r0, C), :]
        k = ks[pl.ds(r0, C), :]
        v = vs[pl.ds(r0, C), :]
        gb = gs[pl.ds(r0, C), :]
        d = 0 if fwd else 1
        incl = (row >= col) if fwd else (row <= col)
        g_blk = jnp.dot(incl.astype(F32), gb, precision=HIGHEST, preferred_element_type=F32)
        g_lane = 2 * GDN_HEADS + d * GDN_HEADS + head
        b_lane = d * GDN_HEADS + head
        g_col = jnp.sum(jnp.where(lane == g_lane, g_blk, 0.0), axis=-1, keepdims=True)
        b_col = jnp.sum(jnp.where(lane == b_lane, gb, 0.0), axis=-1, keepdims=True)
        g_row = jnp.sum(jnp.where(sub_t == g_lane, g_blk.T, 0.0), axis=0, keepdims=True)
        decay = jnp.exp(jnp.where(incl, g_col - g_row, -jnp.inf))
        qk_kk = lax.dot_general(jnp.concatenate([q, k], axis=0).astype(BF16), k.astype(BF16), _NT,
                                preferred_element_type=F32)
        attn = qk_kk[:C] * decay
        low = qk_kk[C:] * b_col * decay
        t_inv = eye
        g = 1
        while g < C:
            same = (row // (2 * g)) == (col // (2 * g))
            hi_r, hi_c = (row % (2 * g)) >= g, (col % (2 * g)) >= g
            off_diag = same & ((hi_r & ~hi_c) if fwd else (~hi_r & hi_c))
            x = jnp.where(off_diag, low, 0.0)
            t_inv = t_inv - (x if g == 1 else _bdot(t_inv, _bdot(x, t_inv)))
            g *= 2
        e_g = jnp.exp(g_col)
        w = _bdot(jnp.concatenate([q * e_g, k * e_g], axis=0), S)
        v_new = _bdot(t_inv, b_col * (v - w[C:]))
        o = w[:C] + _bdot(attn, v_new)
        g_last = g_col[C - 1:C, :] if fwd else g_col[0:1, :]
        kg = k * jnp.exp(g_last - g_col)
        S = S * jnp.exp(g_last) + lax.dot_general(kg.astype(BF16), v_new.astype(BF16), _TN,
                                                  preferred_element_type=F32)
        oacc[pl.ds(r0, C), :] += o
        return S

    def step(s, carry):
        s_f, s_b = carry
        cb = jnp.where(s < n_ctx, n_ctx - 1 - s, n_tot - 1 - (s - n_ctx))
        return chunk(s, s_f, True), chunk(cb, s_b, False)

    zero = jnp.zeros((hd, hd), F32)
    lax.fori_loop(0, n_tot, step, (zero, zero))

    def finish(z_ref, y_ref, off, n_rows):
        def body(r, carry):
            start = pl.multiple_of(r * CONV_ROWS, CONV_ROWS)
            o = oacc[pl.ds(off + start, CONV_ROWS), :]
            y = _rms(o, nw_ref[...]) * jax.nn.silu(z_ref[pl.ds(start, CONV_ROWS), :])
            y_ref[pl.ds(start, CONV_ROWS), :] = y.astype(y_ref.dtype)
            return carry

        lax.fori_loop(0, n_rows // CONV_ROWS, body, 0)

    finish(zc_ref, yc_ref, 0, ctx)
    finish(zl_ref, yl_ref, ctx, seq)


def _gdn_call(p, bg, conv_w, ald, norm_w, *, batch, seq, ctx):
    H = GDN_HEADS
    cb0 = batch * seq // ctx
    lat = lambda cblk: pl.BlockSpec((seq, LANES), lambda b, h: (b, cblk * H + h))
    cx = lambda cblk: pl.BlockSpec((ctx, LANES), lambda b, h: (cb0 + b, cblk * H + h))
    cw = lambda cblk: pl.BlockSpec((CONV_WIDTH, LANES), lambda b, h: (0, cblk * H + h))
    T = seq + ctx
    return pl.pallas_call(
        functools.partial(_gdn_kernel, seq=seq, ctx=ctx),
        out_shape=(jax.ShapeDtypeStruct((batch * seq, GDN_WIDTH), BF16),
                   jax.ShapeDtypeStruct((batch * ctx, GDN_WIDTH), BF16)),
        grid=(batch, H),
        in_specs=[lat(0), lat(1), lat(2), cx(0), cx(1), cx(2), lat(3), cx(3),
                  pl.BlockSpec((seq, LANES), lambda b, h: (b, 0)),
                  pl.BlockSpec((ctx, LANES), lambda b, h: (cb0 + b, 0)),
                  cw(0), cw(1), cw(2),
                  pl.BlockSpec((2, LANES), lambda b, h: (0, 0)),
                  pl.BlockSpec((1, LANES), lambda b, h: (0, 0))],
        out_specs=(pl.BlockSpec((seq, LANES), lambda b, h: (b, h)),
                   pl.BlockSpec((ctx, LANES), lambda b, h: (b, h))),
        scratch_shapes=[pltpu.VMEM((T, LANES), F32)] * 5,
        compiler_params=_params("arbitrary", "arbitrary"),
        name="gdn",
    )(p, p, p, p, p, p, p, p, bg, bg, conv_w, conv_w, conv_w, ald, norm_w)


def _lru_scan(a, u, fwd):
    n = a.shape[0]
    row = lax.broadcasted_iota(jnp.int32, a.shape, 0)
    s = 1
    while s < n:
        shift = s if fwd else n - s
        valid = (row >= s) if fwd else (row < n - s)
        u = jnp.where(valid, a * pltpu.roll(u, shift, axis=0) + u, u)
        a = jnp.where(valid, a * pltpu.roll(a, shift, axis=0), a)
        s *= 2
    return a, u


def _lru_kernel(xl_ref, xc_ref, gl_ref, gc_ref, cw_ref, cb_ref, w_ref, b_ref, lam_ref, yl_ref, yc_ref,
                ab_s, ub_s, hf_s, *, seq, ctx):
    W = LRU_WIDTH
    neg_log_base = jax.nn.softplus(-lam_ref[...])
    cw = cw_ref[...]

    def forward(src_ref, off, n_rows, h):
        n_chunks = n_rows // CONV_ROWS

        def body(r, h):
            start, acc = _conv_chunk(src_ref, r, n_chunks, n_rows, cw)
            xc = acc + cb_ref[...]
            gates = jax.nn.sigmoid(_bdot(xc, w_ref[...]) + b_ref[...])
            coef = []
            for d in range(2):
                rg = gates[:, 2 * d * W:(2 * d + 1) * W]
                ig = gates[:, (2 * d + 1) * W:(2 * d + 2) * W]
                log_a = -LRU_C * rg * neg_log_base[d:d + 1, :]
                a = jnp.exp(log_a)
                coef.append((a, jnp.sqrt(-jnp.tanh(log_a) * (a * a + 1.0)) * (ig * xc)))
            rows = pl.ds(off + start, CONV_ROWS)
            ab_s[rows, :] = coef[1][0]
            ub_s[rows, :] = coef[1][1]
            a_cum, h0 = _lru_scan(coef[0][0], coef[0][1], True)
            hf = h0 + a_cum * h
            hf_s[rows, :] = hf
            return hf[CONV_ROWS - 1:CONV_ROWS, :]

        return lax.fori_loop(0, n_chunks, body, h)

    def backward(gate_ref, y_ref, off, n_rows, h):
        n_chunks = n_rows // CONV_ROWS

        def body(i, h):
            r = n_chunks - 1 - i
            start = pl.multiple_of(r * CONV_ROWS, CONV_ROWS)
            rows = pl.ds(off + start, CONV_ROWS)
            a_cum, h0 = _lru_scan(ab_s[rows, :], ub_s[rows, :], False)
            hb = h0 + a_cum * h
            y = (hf_s[rows, :] + hb) * jax.nn.gelu(gate_ref[pl.ds(start, CONV_ROWS), :])
            y_ref[pl.ds(start, CONV_ROWS), :] = y.astype(y_ref.dtype)
            return hb[0:1, :]

        return lax.fori_loop(0, n_chunks, body, h)

    zero = jnp.zeros((1, W), F32)
    forward(xl_ref, ctx, seq, forward(xc_ref, 0, ctx, zero))
    backward(gl_ref, yl_ref, ctx, seq, backward(gc_ref, yc_ref, 0, ctx, zero))


def _lru_call(p, conv_w, conv_b, w_blk, b_blk, lam, *, batch, seq, ctx):
    cb0 = batch * seq // ctx
    x_col = (3 * GDN_WIDTH + GDN_WIDTH) // LRU_WIDTH
    T = seq + ctx
    full = lambda a: pl.BlockSpec(a.shape, lambda b: (0,) * a.ndim)
    return pl.pallas_call(
        functools.partial(_lru_kernel, seq=seq, ctx=ctx),
        out_shape=(jax.ShapeDtypeStruct((batch * seq, LRU_WIDTH), BF16),
                   jax.ShapeDtypeStruct((batch * ctx, LRU_WIDTH), BF16)),
        grid=(batch,),
        in_specs=[pl.BlockSpec((seq, LRU_WIDTH), lambda b: (b, x_col)),
                  pl.BlockSpec((ctx, LRU_WIDTH), lambda b: (cb0 + b, x_col)),
                  pl.BlockSpec((seq, LRU_WIDTH), lambda b: (b, x_col + 1)),
                  pl.BlockSpec((ctx, LRU_WIDTH), lambda b: (cb0 + b, x_col + 1)),
                  full(conv_w), full(conv_b), full(w_blk), full(b_blk), full(lam)],
        out_specs=(pl.BlockSpec((seq, LRU_WIDTH), lambda b: (b, 0)),
                   pl.BlockSpec((ctx, LRU_WIDTH), lambda b: (b, 0))),
        scratch_shapes=[pltpu.VMEM((T, LRU_WIDTH), F32)] * 3,
        compiler_params=_params("arbitrary"),
        name="rglru",
    )(p, p, p, p, conv_w, conv_b, w_blk, b_blk, lam)


def _attn_kernel(*refs, n_src):
    q_ref = refs[0]
    srcs = [(refs[1 + 2 * i], refs[2 + 2 * i]) for i in range(n_src)]
    o_ref = refs[1 + 2 * n_src]
    s_scr = refs[2 + 2 * n_src]
    tq = q_ref.shape[0]
    lane = lax.broadcasted_iota(jnp.int32, (1, MLA_WIDTH), 1)
    out = jnp.zeros((tq, MLA_WIDTH), F32)
    for h in range(MLA_HEADS):
        hcols = slice(h * LANES, (h + 1) * LANES)
        qh = q_ref[:, hcols]
        m_part = jnp.full((tq, LANES), -jnp.inf, F32)
        off = 0
        for k_ref, _ in srcs:
            def scores(j, mp, k_ref=k_ref, off=off):
                kb = k_ref[pl.ds(pl.multiple_of(j * ATT_KB, ATT_KB), ATT_KB), hcols]
                s = lax.dot_general(qh, kb, _NT, preferred_element_type=F32)
                s_scr[:, pl.ds(pl.multiple_of(off + j * ATT_KB, ATT_KB), ATT_KB)] = s
                return jnp.maximum(mp, jnp.maximum(s[:, :LANES], s[:, LANES:]))

            m_part = lax.fori_loop(0, k_ref.shape[0] // ATT_KB, scores, m_part)
            off += k_ref.shape[0]
        m = jnp.max(m_part, axis=-1, keepdims=True)
        head_lanes = (lane >= h * MLA_V) & (lane < (h + 1) * MLA_V)
        carry = (jnp.zeros((tq, LANES), F32), jnp.zeros((tq, MLA_WIDTH), F32))
        off = 0
        for _, v_ref in srcs:
            def weighted(j, c, v_ref=v_ref, off=off):
                l_part, acc = c
                s = s_scr[:, pl.ds(pl.multiple_of(off + j * ATT_KB, ATT_KB), ATT_KB)]
                p = jnp.exp(s - m)
                vb = v_ref[pl.ds(pl.multiple_of(j * ATT_KB, ATT_KB), ATT_KB), :]
                vb = jnp.where(head_lanes, vb, jnp.zeros_like(vb))
                acc = acc + jnp.dot(p.astype(BF16), vb, preferred_element_type=F32)
                return l_part + p[:, :LANES] + p[:, LANES:], acc

            carry = lax.fori_loop(0, v_ref.shape[0] // ATT_KB, weighted, carry)
            off += v_ref.shape[0]
        out = out + carry[1] * (1.0 / jnp.sum(carry[0], axis=-1, keepdims=True))
    o_ref[...] = out.astype(o_ref.dtype)


def _attn_call(q, k, v, *, batch, seq, ctx, latent):
    hw = MLA_HEADS * LANES
    cb0 = batch * seq // ctx
    if latent:
        per_b = seq // ATT_TQ
        grid = (batch, per_b)
        q_spec = pl.BlockSpec((ATT_TQ, hw), lambda b, i: (b * per_b + i, 0))
        kv_specs = [pl.BlockSpec((ctx, hw), lambda b, i: (cb0 + b, 0)),
                    pl.BlockSpec((ctx, MLA_WIDTH), lambda b, i: (cb0 + b, 0)),
                    pl.BlockSpec((seq, hw), lambda b, i: (b, 0)),
                    pl.BlockSpec((seq, MLA_WIDTH), lambda b, i: (b, 0))]
        args = (q, k, v, k, v)
        out_spec = pl.BlockSpec((ATT_TQ, MLA_WIDTH), lambda b, i: (b * per_b + i, 0))
        rows, tq, keys, n_src = batch * seq, ATT_TQ, ctx + seq, 2
    else:
        grid = (batch, 1)
        q_spec = pl.BlockSpec((ctx, hw), lambda b, i: (cb0 + b, 0))
        kv_specs = [pl.BlockSpec((ctx, hw), lambda b, i: (cb0 + b, 0)),
                    pl.BlockSpec((ctx, MLA_WIDTH), lambda b, i: (cb0 + b, 0))]
        args = (q, k, v)
        out_spec = pl.BlockSpec((ctx, MLA_WIDTH), lambda b, i: (b, 0))
        rows, tq, keys, n_src = batch * ctx, ctx, ctx, 1
    return pl.pallas_call(
        functools.partial(_attn_kernel, n_src=n_src),
        out_shape=jax.ShapeDtypeStruct((rows, MLA_WIDTH), BF16),
        grid=grid,
        in_specs=[q_spec] + kv_specs,
        out_specs=out_spec,
        scratch_shapes=[pltpu.VMEM((tq, keys), F32)],
        compiler_params=_params("arbitrary", "arbitrary"),
        name="mla_lat" if latent else "mla_ctx",
    )(*args)


def _outmlp_kernel(*refs, with_ctx, n_lat_blocks):
    x_ref, mod_ref = refs[0], refs[1]
    n_y = 6 if with_ctx else 3
    y_refs = refs[2:2 + n_y]
    gpost_ref, gpre_ref, gmlp_ref, wo_ref, w1_ref, w2_ref, o_ref = refs[2 + n_y:]
    if with_ctx:
        is_lat = pl.program_id(0) < n_lat_blocks
        parts = [jnp.where(is_lat, y_refs[2 * i][...], y_refs[2 * i + 1][...]) for i in range(3)]
    else:
        parts = [r[...] for r in y_refs]
    y = jnp.concatenate(parts, axis=1)
    t = jnp.dot(y, wo_ref[...], preferred_element_type=F32)
    x1 = x_ref[...] + mod_ref[2:3, :] * _rms(t, gpost_ref[...])
    h2 = (_rms(x1, gpre_ref[...]) * (1.0 + mod_ref[4:5, :]) + mod_ref[3:4, :]).astype(BF16)
    acc = jnp.zeros(x1.shape, F32)
    for f in range(D_FF // FF_CHUNK):
        cols = slice(f * FF_CHUNK, (f + 1) * FF_CHUNK)
        a = jnp.dot(h2, w1_ref[:, cols], preferred_element_type=F32)
        a = jnp.square(jnp.maximum(a, 0.0)).astype(BF16)
        acc = acc + jnp.dot(a, w2_ref[cols, :], preferred_element_type=F32)
    o_ref[...] = x1 + mod_ref[5:6, :] * _rms(acc, gmlp_ref[...])


def _outmlp_call(xs, mods_l, ys, gpost, gpre, gmlp, wo, w1, w2, *, batch, seq, with_ctx):
    n = xs.shape[0]
    nlb = batch * seq // TM
    nblk = n // TM if with_ctx else nlb
    per_b = seq // TM
    modrow = lambda i: jnp.where(i < nlb, i // per_b, batch)
    rowblk = lambda w: pl.BlockSpec((TM, w), lambda i: (i, 0))
    resident = lambda a: pl.BlockSpec(a.shape, lambda i: (0,) * a.ndim, pipeline_mode=pl.Buffered(1))
    y_specs, y_args = [], []
    for y_lat, y_ctx in ys:
        w = y_lat.shape[1]
        y_specs.append(pl.BlockSpec((TM, w), lambda i: (jnp.minimum(i, nlb - 1), 0)))
        y_args.append(y_lat)
        if with_ctx:
            y_specs.append(pl.BlockSpec((TM, w), lambda i: (jnp.maximum(i - nlb, 0), 0)))
            y_args.append(y_ctx)
    return pl.pallas_call(
        functools.partial(_outmlp_kernel, with_ctx=with_ctx, n_lat_blocks=nlb),
        out_shape=jax.ShapeDtypeStruct((n, D_MODEL), F32),
        grid=(nblk,),
        in_specs=[rowblk(D_MODEL), pl.BlockSpec((None, 6, D_MODEL), lambda i: (modrow(i), 0, 0))] + y_specs
                 + [resident(gpost), resident(gpre), resident(gmlp), resident(wo), resident(w1), resident(w2)],
        out_specs=rowblk(D_MODEL),
        compiler_params=_params("arbitrary"),
        name="outproj_mlp",
    )(xs, mods_l, *y_args, gpost, gpre, gmlp, wo, w1, w2)


def _rope_rot_cols(w):
    w4 = w.reshape(w.shape[:-1] + (2, 2, MLA_ROPE // 4))
    return jnp.stack([-w4[..., 1, :], w4[..., 0, :]], axis=-2).reshape(w.shape)


def _zeros_like_cols(w, n):
    return jnp.zeros(w.shape[:-1] + (n,), w.dtype)


def _layout_w_in(w_in):
    o = np.concatenate([[0], np.cumsum(IN_SIZES)])
    qkv, z, beta, dec, lx, lg, cq, ckv, kr = [w_in[..., o[i]:o[i + 1]] for i in range(len(IN_SIZES))]
    zc = functools.partial(_zeros_like_cols, w_in)
    main = jnp.concatenate([qkv, z, lx, lg], axis=-1)
    pad = LANES - MLA_NOPE - MLA_ROPE
    rest = jnp.concatenate([cq, ckv, beta, dec, zc(LANES - 4 * GDN_HEADS),
                            zc(MLA_NOPE), kr, zc(pad), zc(MLA_NOPE), _rope_rot_cols(kr), zc(pad)], axis=-1)
    return main.astype(BF16), rest.astype(BF16)


def _layout_mla(w_uq, w_ukv):
    lead = w_uq.shape[:-1]
    q4 = w_uq.reshape(lead + (MLA_HEADS, MLA_NOPE + MLA_ROPE))
    nope, rope = q4[..., :MLA_NOPE], q4[..., MLA_NOPE:]
    pad = LANES - MLA_NOPE - MLA_ROPE
    wa = jnp.concatenate([nope, rope, _zeros_like_cols(rope, pad)], axis=-1)
    wb = jnp.concatenate([jnp.zeros_like(nope), _rope_rot_cols(rope), _zeros_like_cols(rope, pad)], axis=-1)
    wq = jnp.concatenate([wa.reshape(lead + (-1,)), wb.reshape(lead + (-1,))], axis=-1)
    lead = w_ukv.shape[:-1]
    kv4 = w_ukv.reshape(lead + (MLA_HEADS, MLA_NOPE + MLA_V))
    k_nope, v = kv4[..., :MLA_NOPE], kv4[..., MLA_NOPE:]
    wkn = jnp.concatenate([k_nope, _zeros_like_cols(k_nope, LANES - MLA_NOPE)], axis=-1).reshape(lead + (-1,))
    return wq.astype(BF16), wkn.astype(BF16), v.reshape(lead + (-1,)).astype(BF16)


def _rope_table(seq):
    rows = seq // GRID_W
    row = jnp.repeat(jnp.arange(rows, dtype=F32), GRID_W)
    col = jnp.tile(jnp.arange(GRID_W, dtype=F32), rows)
    half = MLA_ROPE // 2
    inv = ROPE_BASE ** (-jnp.arange(0, half, 2, dtype=F32) / half)
    ang = jnp.stack([row[:, None] * inv, col[:, None] * inv], axis=1)
    ang = jnp.concatenate([ang, ang], axis=-1).reshape(seq, MLA_ROPE)
    cos = jnp.concatenate([jnp.cos(ang), jnp.ones((TM, MLA_ROPE), F32)], axis=0)
    sin = jnp.concatenate([jnp.sin(ang), jnp.zeros((TM, MLA_ROPE), F32)], axis=0)
    n = seq + TM
    one, zero = jnp.ones((n, MLA_NOPE), F32), jnp.zeros((n, MLA_NOPE), F32)
    pad = jnp.zeros((n, LANES - MLA_NOPE - MLA_ROPE), F32)
    scale = (MLA_NOPE + MLA_ROPE) ** -0.5
    return jnp.concatenate([scale * one, scale * cos, pad, zero, scale * sin, pad,
                            zero, cos, pad, zero, sin, pad], axis=1)


def _layout_lru(w_a, b_a, w_i, b_i):
    def blockdiag(w):
        g, n = w.shape[1], w.shape[2]
        out = jnp.zeros((w.shape[0], g * n, g * n), w.dtype)
        for j in range(g):
            out = out.at[:, j * n:(j + 1) * n, j * n:(j + 1) * n].set(w[:, j])
        return out

    da, di = blockdiag(w_a), blockdiag(w_i)
    w = jnp.concatenate([da[0], di[0], da[1], di[1]], axis=-1).astype(BF16)
    b = jnp.concatenate([b_a[0], b_i[0], b_a[1], b_i[1]], axis=-1)[None, :]
    return w, b


def kernel(x, c, ctx, c_ctx, w_ada, b_ada, g_attn_pre, g_attn_post, g_mlp_pre, g_mlp_post, w_in, gdn_conv_w,
           gdn_a_log, gdn_dt_bias, gdn_norm_w, lru_conv_w, lru_conv_b, lru_w_a, lru_b_a, lru_w_i, lru_b_i,
           lru_lambda, mla_q_norm, mla_w_uq, mla_kv_norm, mla_w_ukv, w_out, w_mlp1, w_mlp2):
    batch, seq, _ = x.shape
    n_ctx = ctx.shape[1]
    depth = w_ada.shape[0]
    assert seq % TM == 0 and (batch * n_ctx) % TM == 0 and n_ctx % CONV_ROWS == 0 and seq % GRID_W == 0
    assert (batch * seq) % n_ctx == 0 and batch + 1 <= 2 * SUBLANES
    dims = dict(batch=batch, seq=seq, ctx=n_ctx)

    cond = jnp.concatenate([c, c_ctx[None, :], jnp.zeros((2 * SUBLANES - batch - 1, D_MODEL), F32)], axis=0)
    mods = _ada_call(cond, w_ada, b_ada).reshape(depth, 2 * SUBLANES, 6, D_MODEL)
    tab = _rope_table(seq)
    xs = jnp.concatenate([x.reshape(batch * seq, D_MODEL), ctx.reshape(batch * n_ctx, D_MODEL)], axis=0)
    row = lambda a: a[None, :]

    for l in range(depth):
        last = l == depth - 1
        wm, wr = _layout_w_in(w_in[l])
        wq, wkn, wv = _layout_mla(mla_w_uq[l], mla_w_ukv[l])
        p, bg, q, k, v = _inproj_call(xs, mods[l], row(g_attn_pre[l]), wm, wr, row(mla_q_norm[l]), wq,
                                      row(mla_kv_norm[l]), wkn, wv, tab, batch=batch, seq=seq)
        pad = jnp.zeros((LANES - 4 * GDN_HEADS,), F32)
        ald = jnp.stack([jnp.concatenate([jnp.zeros((2 * GDN_HEADS,), F32), gdn_a_log[l].reshape(-1), pad]),
                         jnp.concatenate([jnp.zeros((2 * GDN_HEADS,), F32), gdn_dt_bias[l].reshape(-1), pad])])
        y_gdn = _gdn_call(p, bg, gdn_conv_w[l], ald, row(gdn_norm_w[l]), **dims)
        w_blk, b_blk = _layout_lru(lru_w_a[l], lru_b_a[l], lru_w_i[l], lru_b_i[l])
        y_lru = _lru_call(p, lru_conv_w[l], row(lru_conv_b[l]), w_blk, b_blk, lru_lambda[l], **dims)
        y_att = (_attn_call(q, k, v, latent=True, **dims),
                 None if last else _attn_call(q, k, v, latent=False, **dims))
        xs = _outmlp_call(xs, mods[l], (y_gdn, y_lru, y_att), row(g_attn_post[l]), row(g_mlp_pre[l]),
                          row(g_mlp_post[l]), w_out[l].astype(BF16), w_mlp1[l].astype(BF16),
                          w_mlp2[l].astype(BF16), batch=batch, seq=seq, with_ctx=not last)
    return xs[:batch * seq].reshape(batch, seq, D_MODEL)
```

```python
import functools
import math

import numpy as np
import jax
import jax.numpy as jnp
from jax import lax
from jax.experimental import pallas as pl
from jax.experimental.pallas import tpu as pltpu

F32 = jnp.float32
BF16 = jnp.bfloat16
HIGHEST = lax.Precision.HIGHEST

D_MODEL = 1024
EPS = 1e-6
GRID_W = 64
CONV_WIDTH = 4
GDN_HEAD_DIM = 128
GDN_HEADS = 4
GDN_WIDTH = GDN_HEADS * GDN_HEAD_DIM
GDN_CHUNK = 64
LRU_WIDTH = 256
LRU_BLOCKS = 4
LRU_C = 8.0
MLA_V = 64
MLA_HEADS = 4
MLA_WIDTH = MLA_HEADS * MLA_V
MLA_NOPE = 64
MLA_ROPE = 32
MLA_Q_RANK = 256
MLA_KV_RANK = 128
ROPE_BASE = 10000.0
D_FF = 4 * D_MODEL
IN_SIZES = (3 * GDN_WIDTH, GDN_WIDTH, 2 * GDN_HEADS, 2 * GDN_HEADS, LRU_WIDTH, LRU_WIDTH,
            MLA_Q_RANK, MLA_KV_RANK, MLA_ROPE)

LANES = 128
SUBLANES = 8
TM = 512
MAIN_COLS = 3 * GDN_WIDTH + GDN_WIDTH + 2 * LRU_WIDTH
REST_COLS = MLA_Q_RANK + MLA_KV_RANK + 3 * LANES
CONV_ROWS = 256
ATT_TQ = 256
ATT_KB = 512
FF_CHUNK = 1024
VMEM_LIMIT = 56 * 1024 * 1024

_NT = (((1,), (1,)), ((), ()))
_TN = (((0,), (0,)), ((), ()))


def _rms(x, g):
    return x * lax.rsqrt(jnp.mean(x * x, axis=-1, keepdims=True) + EPS) * g


def _bdot(a, b):
    return jnp.dot(a.astype(BF16), b.astype(BF16), preferred_element_type=F32)


def _params(*sem):
    return pltpu.CompilerParams(dimension_semantics=sem, vmem_limit_bytes=VMEM_LIMIT)


def _ada_kernel(c_ref, w_ref, b_ref, o_ref):
    o_ref[...] = _bdot(jax.nn.silu(c_ref[...]), w_ref[...]) + b_ref[...]


def _ada_call(cond, w_ada, b_ada):
    depth = w_ada.shape[0]
    rows = cond.shape[0]
    return pl.pallas_call(
        _ada_kernel,
        out_shape=jax.ShapeDtypeStruct((depth, rows, 6 * D_MODEL), F32),
        grid=(depth, 6),
        in_specs=[pl.BlockSpec((rows, D_MODEL), lambda l, j: (0, 0)),
                  pl.BlockSpec((None, D_MODEL, D_MODEL), lambda l, j: (l, 0, j)),
                  pl.BlockSpec((None, 1, D_MODEL), lambda l, j: (l, 0, j))],
        out_specs=pl.BlockSpec((None, rows, D_MODEL), lambda l, j: (l, 0, j)),
        compiler_params=_params("arbitrary", "arbitrary"),
        name="adaln",
    )(cond, w_ada, b_ada.reshape(depth, 1, 6 * D_MODEL))


def _inproj_kernel(x_ref, mod_ref, g_ref, wm_ref, wr_ref, qn_ref, wq_ref, kvn_ref, wkn_ref, wv_ref, tab_ref,
                   p_ref, bg_ref, q_ref, kt_ref, v_ref):
    h = _rms(x_ref[...], g_ref[...]) * (1.0 + mod_ref[1:2, :]) + mod_ref[0:1, :]
    hb = h.astype(BF16)
    p_ref[...] = jnp.dot(hb, wm_ref[...], preferred_element_type=F32)
    r = jnp.dot(hb, wr_ref[...], preferred_element_type=F32)
    o0 = MLA_Q_RANK
    o1 = o0 + MLA_KV_RANK
    cq = r[:, :o0]
    ckv = r[:, o0:o1]
    bg_ref[...] = r[:, o1:o1 + LANES]
    kr_a = r[:, o1 + LANES:o1 + 2 * LANES]
    kr_b = r[:, o1 + 2 * LANES:o1 + 3 * LANES]
    tab = tab_ref[...]
    tile = lambda t: jnp.concatenate([t] * MLA_HEADS, axis=1)
    q2 = _bdot(_rms(cq, qn_ref[...]), wq_ref[...])
    hw = MLA_HEADS * LANES
    q_ref[...] = (q2[:, :hw] * tile(tab[:, :LANES]) + q2[:, hw:] * tile(tab[:, LANES:2 * LANES])).astype(BF16)
    ckvn = _rms(ckv, kvn_ref[...]).astype(BF16)
    kr = kr_a * tab[:, 2 * LANES:3 * LANES] + kr_b * tab[:, 3 * LANES:]
    kt_ref[...] = (jnp.dot(ckvn, wkn_ref[...], preferred_element_type=F32) + tile(kr)).T.astype(BF16)
    v_ref[...] = jnp.dot(ckvn, wv_ref[...], preferred_element_type=F32).astype(BF16)


def _inproj_call(xs, mods_l, g, wm, wr, qn, wq, kvn, wkn, wv, tab, *, batch, seq):
    n = xs.shape[0]
    nblk = n // TM
    nlb = batch * seq // TM
    per_b = seq // TM
    modrow = lambda i: jnp.where(i < nlb, i // per_b, batch)
    tabrow = lambda i: jnp.where(i < nlb, i % per_b, per_b)
    full = lambda a: pl.BlockSpec(a.shape, lambda i: (0,) * a.ndim)
    rowblk = lambda w: pl.BlockSpec((TM, w), lambda i: (i, 0))
    hw = MLA_HEADS * LANES
    return pl.pallas_call(
        _inproj_kernel,
        out_shape=(jax.ShapeDtypeStruct((n, MAIN_COLS), F32),
                   jax.ShapeDtypeStruct((n, LANES), F32),
                   jax.ShapeDtypeStruct((n, hw), BF16),
                   jax.ShapeDtypeStruct((hw, n), BF16),
                   jax.ShapeDtypeStruct((n, MLA_WIDTH), BF16)),
        grid=(nblk,),
        in_specs=[rowblk(D_MODEL),
                  pl.BlockSpec((None, 6, D_MODEL), lambda i: (modrow(i), 0, 0)),
                  full(g), full(wm), full(wr), full(qn), full(wq), full(kvn), full(wkn), full(wv),
                  pl.BlockSpec((TM, 4 * LANES), lambda i: (tabrow(i), 0))],
        out_specs=(rowblk(MAIN_COLS), rowblk(LANES), rowblk(hw), pl.BlockSpec((hw, TM), lambda i: (0, i)),
                   rowblk(MLA_WIDTH)),
        compiler_params=_params("arbitrary"),
        name="inproj",
    )(xs, mods_l, g, wm, wr, qn, wq, kvn, wkn, wv, tab)


def _conv_chunk(src_ref, r, n_chunks, n_rows, w, rows=CONV_ROWS):
    start = pl.multiple_of(r * rows, rows)
    cur = src_ref[pl.ds(start, rows), :]
    prev_start = pl.multiple_of(jnp.maximum(start - SUBLANES, 0), SUBLANES)
    prev = jnp.where(r > 0, src_ref[pl.ds(prev_start, SUBLANES), :], 0.0)
    next_start = pl.multiple_of(jnp.minimum(start + rows, n_rows - SUBLANES), SUBLANES)
    nxt = jnp.where(r < n_chunks - 1, src_ref[pl.ds(next_start, SUBLANES), :], 0.0)
    cat = jnp.concatenate([prev, cur, nxt], axis=0)
    tot = rows + 2 * SUBLANES
    sl = slice(SUBLANES, SUBLANES + rows)
    acc = w[2:3, :] * cur
    acc = acc + w[0:1, :] * pltpu.roll(cat, 2, axis=0)[sl]
    acc = acc + w[1:2, :] * pltpu.roll(cat, 1, axis=0)[sl]
    acc = acc + w[3:4, :] * pltpu.roll(cat, tot - 1, axis=0)[sl]
    return start, acc


GDN_UNROLL = 4


def _gdn_kernel(ql_ref, kl_ref, vl_ref, qc_ref, kc_ref, vc_ref, zl_ref, zc_ref, bgl_ref, bgc_ref,
                cwq_ref, cwk_ref, cwv_ref, ald_ref, nw_ref, yl_ref, yc_ref,
                lhs_s, upd_s, gl_s, oacc, *, seq, ctx):
    head = pl.program_id(1)
    C = GDN_CHUNK
    C2 = 2 * C
    hd = GDN_HEAD_DIM
    n_ctx = ctx // C
    n_tot = (ctx + seq) // C

    lane = lax.broadcasted_iota(jnp.int32, (C, LANES), 1)
    sub_t = lax.broadcasted_iota(jnp.int32, (LANES, C), 0)
    row = lax.broadcasted_iota(jnp.int32, (C, C), 0)
    col = lax.broadcasted_iota(jnp.int32, (C, C), 1)
    tri = ((row >= col).astype(F32), (row <= col).astype(F32))
    prow = lax.broadcasted_iota(jnp.int32, (C2, C2), 0)
    pcol = lax.broadcasted_iota(jnp.int32, (C2, C2), 1)
    sgn = 1 - 2 * (prow // C)
    incl_p = ((prow // C) == (pcol // C)) & ((prow - pcol) * sgn >= 0)
    eye_p = (prow == pcol).astype(F32)
    off_diag = []
    g = 1
    while g < C:
        half = (prow % (2 * g)) // g - (pcol % (2 * g)) // g
        off_diag.append(((prow // (2 * g)) == (pcol // (2 * g))) & (half * sgn == 1))
        g *= 2
    a_log = ald_ref[0:1, :]
    dt_b = ald_ref[1:2, :]
    cws = (cwq_ref[...], cwk_ref[...], cwv_ref[...])
    dup_rows = lambda t: jnp.concatenate([t, t], axis=0)

    def local_chunks(srcs, bg_ref, rs, n_chunks, n_rows, c_off):
        U = range(len(rs))
        q, k, v, bg = [], [], [], []
        for r in rs:
            qkv = []
            for (src, w, kind) in zip(srcs, cws, "qkv"):
                start, acc = _conv_chunk(src, r, n_chunks, n_rows, w, rows=C)
                y = jax.nn.silu(acc)
                if kind != "v":
                    y = y * lax.rsqrt(jnp.sum(y * y, axis=-1, keepdims=True) + EPS)
                if kind == "q":
                    y = y * (hd ** -0.5)
                qkv.append(y)
            q.append(qkv[0]); k.append(qkv[1]); v.append(qkv[2])
            bg.append(bg_ref[pl.ds(start, C), :])
        beta = [jax.nn.sigmoid(t) for t in bg]
        g_dec = [-jnp.exp(a_log) * jax.nn.softplus(t + dt_b) for t in bg]
        g_blk = [[jnp.dot(tri[d], g_dec[u], precision=HIGHEST, preferred_element_type=F32) for d in range(2)]
                 for u in U]
        r_qk = [lax.dot_general(jnp.concatenate([q[u], k[u]], axis=0).astype(BF16), dup_rows(k[u].astype(BF16)),
                                _NT, preferred_element_type=F32) for u in U]
        g_col, b_col, g_last, decay = [], [], [], []
        for u in U:
            g_cols, b_cols, g_rows, g_lasts = [], [], [], []
            for d in range(2):
                g_lane = 2 * GDN_HEADS + d * GDN_HEADS + head
                gc = jnp.sum(jnp.where(lane == g_lane, g_blk[u][d], 0.0), axis=-1, keepdims=True)
                g_cols.append(gc)
                b_cols.append(jnp.sum(jnp.where(lane == d * GDN_HEADS + head, beta[u], 0.0), axis=-1,
                                      keepdims=True))
                g_rows.append(jnp.sum(jnp.where(sub_t == g_lane, g_blk[u][d].T, 0.0), axis=0, keepdims=True))
                g_lasts.append(gc[C - 1:C, :] if d == 0 else gc[0:1, :])
            g_col.append(jnp.concatenate(g_cols, axis=0))
            b_col.append(jnp.concatenate(b_cols, axis=0))
            g_last.append(g_lasts)
            g_row = jnp.concatenate(g_rows, axis=1)
            decay.append(jnp.exp(jnp.where(incl_p, g_col[u] - g_row, -jnp.inf)))
        attn = [dup_rows(r_qk[u][:C]) * decay[u] for u in U]
        low = [dup_rows(r_qk[u][C:]) * b_col[u] * decay[u] for u in U]
        t_inv = [eye_p - jnp.where(off_diag[0], low[u], 0.0) for u in U]
        for m in off_diag[1:]:
            xt = [_bdot(jnp.where(m, low[u], 0.0), t_inv[u]) for u in U]
            t_inv = [t_inv[u] - _bdot(t_inv[u], xt[u]) for u in U]
        e_g = [jnp.exp(g_col[u]) for u in U]
        x = [_bdot(t_inv[u], jnp.concatenate([dup_rows(k[u]) * (e_g[u] * b_col[u]), dup_rows(v[u]) * b_col[u]],
                                             axis=1)) for u in U]
        ax = [_bdot(attn[u], x[u]) for u in U]
        zero = jnp.zeros((C, hd), F32)
        kx = []
        for u in U:
            gl_col = jnp.concatenate([jnp.broadcast_to(g_last[u][0], (C, 1)),
                                      jnp.broadcast_to(g_last[u][1], (C, 1))], axis=0)
            kg = dup_rows(k[u]) * jnp.exp(gl_col - g_col[u])
            kg_blk = jnp.concatenate([jnp.concatenate([kg[:C], zero], axis=1),
                                      jnp.concatenate([zero, kg[C:]], axis=1)], axis=0)
            kx.append(lax.dot_general(kg_blk.astype(BF16), x[u].astype(BF16), _TN,
                                      preferred_element_type=F32))
        for u in U:
            c_glob = c_off + rs[u]
            q_eff = dup_rows(q[u]) * e_g[u] - ax[u][:, :hd]
            for d in range(2):
                lhs_s[d, c_glob, 0:C, :] = q_eff[d * C:(d + 1) * C].astype(BF16)
                lhs_s[d, c_glob, C:C + hd, :] = kx[u][d * hd:(d + 1) * hd, :hd].astype(BF16)
                upd_s[d, c_glob] = kx[u][d * hd:(d + 1) * hd, hd:]
                gl_s[d, c_glob] = jnp.broadcast_to(jnp.exp(g_last[u][d]), (SUBLANES, LANES))
            oacc[pl.ds(pl.multiple_of(c_glob * C, C), C), :] = ax[u][:C, hd:] + ax[u][C:, hd:]

    def local_segment(srcs, bg_ref, n_rows, c_off):
        n_chunks = n_rows // C

        def body(i, carry):
            local_chunks(srcs, bg_ref, [i * GDN_UNROLL + u for u in range(GDN_UNROLL)], n_chunks, n_rows, c_off)
            return carry

        lax.fori_loop(0, n_chunks // GDN_UNROLL, body, 0)

    local_segment((qc_ref, kc_ref, vc_ref), bgc_ref, ctx, 0)
    local_segment((ql_ref, kl_ref, vl_ref), bgl_ref, seq, n_ctx)

    def step(s, carry):
        chunks = (s, jnp.where(s < n_ctx, n_ctx - 1 - s, n_tot - 1 - (s - n_ctx)))
        new = []
        for d in range(2):
            c = chunks[d]
            res = jnp.dot(lhs_s[d, c], carry[d].astype(BF16), preferred_element_type=F32)
            rows = pl.ds(pl.multiple_of(c * C, C), C)
            oacc[rows, :] = oacc[rows, :] + res[:C]
            new.append(carry[d] * gl_s[d, c, 0:1, :] + upd_s[d, c] - res[C:])
        return tuple(new)

    zero = jnp.zeros((hd, hd), F32)
    lax.fori_loop(0, n_tot, step, (zero, zero))

    def finish(z_ref, y_ref, off, n_rows):
        def body(r, carry):
            start = pl.multiple_of(r * CONV_ROWS, CONV_ROWS)
            o = oacc[pl.ds(off + start, CONV_ROWS), :]
            y = _rms(o, nw_ref[...]) * jax.nn.silu(z_ref[pl.ds(start, CONV_ROWS), :])
            y_ref[pl.ds(start, CONV_ROWS), :] = y.astype(y_ref.dtype)
            return carry

        lax.fori_loop(0, n_rows // CONV_ROWS, body, 0)

    finish(zc_ref, yc_ref, 0, ctx)
    finish(zl_ref, yl_ref, ctx, seq)


def _gdn_call(p, bg, conv_w, ald, norm_w, *, batch, seq, ctx):
    H = GDN_HEADS
    C = GDN_CHUNK
    cb0 = batch * seq // ctx
    lat = lambda cblk: pl.BlockSpec((seq, LANES), lambda b, h: (b, cblk * H + h))
    cx = lambda cblk: pl.BlockSpec((ctx, LANES), lambda b, h: (cb0 + b, cblk * H + h))
    cw = lambda cblk: pl.BlockSpec((CONV_WIDTH, LANES), lambda b, h: (0, cblk * H + h))
    n_tot = (seq + ctx) // C
    assert (seq // C) % GDN_UNROLL == 0 and (ctx // C) % GDN_UNROLL == 0
    return pl.pallas_call(
        functools.partial(_gdn_kernel, seq=seq, ctx=ctx),
        out_shape=(jax.ShapeDtypeStruct((batch * seq, GDN_WIDTH), BF16),
                   jax.ShapeDtypeStruct((batch * ctx, GDN_WIDTH), BF16)),
        grid=(batch, H),
        in_specs=[lat(0), lat(1), lat(2), cx(0), cx(1), cx(2), lat(3), cx(3),
                  pl.BlockSpec((seq, LANES), lambda b, h: (b, 0)),
                  pl.BlockSpec((ctx, LANES), lambda b, h: (cb0 + b, 0)),
                  cw(0), cw(1), cw(2),
                  pl.BlockSpec((2, LANES), lambda b, h: (0, 0)),
                  pl.BlockSpec((1, LANES), lambda b, h: (0, 0))],
        out_specs=(pl.BlockSpec((seq, LANES), lambda b, h: (b, h)),
                   pl.BlockSpec((ctx, LANES), lambda b, h: (b, h))),
        scratch_shapes=[pltpu.VMEM((2, n_tot, C + GDN_HEAD_DIM, LANES), BF16),
                        pltpu.VMEM((2, n_tot, GDN_HEAD_DIM, LANES), F32),
                        pltpu.VMEM((2, n_tot, SUBLANES, LANES), F32),
                        pltpu.VMEM((seq + ctx, LANES), F32)],
        compiler_params=_params("arbitrary", "arbitrary"),
        name="gdn",
    )(p, p, p, p, p, p, p, p, bg, bg, conv_w, conv_w, conv_w, ald, norm_w)


def _lru_scan(a, u, fwd):
    n = a.shape[0]
    row = lax.broadcasted_iota(jnp.int32, a.shape, 0)
    s = 1
    while s < n:
        shift = s if fwd else n - s
        valid = (row >= s) if fwd else (row < n - s)
        u = jnp.where(valid, a * pltpu.roll(u, shift, axis=0) + u, u)
        a = jnp.where(valid, a * pltpu.roll(a, shift, axis=0), a)
        s *= 2
    return a, u


def _lru_kernel(xl_ref, xc_ref, gl_ref, gc_ref, cw_ref, cb_ref, w_ref, b_ref, lam_ref, yl_ref, yc_ref,
                ab_s, ub_s, hf_s, *, seq, ctx):
    W = LRU_WIDTH
    neg_log_base = jax.nn.softplus(-lam_ref[...])
    cw = cw_ref[...]

    def forward(src_ref, off, n_rows, h):
        n_chunks = n_rows // CONV_ROWS

        def body(r, h):
            start, acc = _conv_chunk(src_ref, r, n_chunks, n_rows, cw)
            xc = acc + cb_ref[...]
            gates = jax.nn.sigmoid(_bdot(xc, w_ref[...]) + b_ref[...])
            coef = []
            for d in range(2):
                rg = gates[:, 2 * d * W:(2 * d + 1) * W]
                ig = gates[:, (2 * d + 1) * W:(2 * d + 2) * W]
                log_a = -LRU_C * rg * neg_log_base[d:d + 1, :]
                a = jnp.exp(log_a)
                coef.append((a, jnp.sqrt(-jnp.tanh(log_a) * (a * a + 1.0)) * (ig * xc)))
            rows = pl.ds(off + start, CONV_ROWS)
            ab_s[rows, :] = coef[1][0]
            ub_s[rows, :] = coef[1][1]
            a_cum, h0 = _lru_scan(coef[0][0], coef[0][1], True)
            hf = h0 + a_cum * h
            hf_s[rows, :] = hf
            return hf[CONV_ROWS - 1:CONV_ROWS, :]

        return lax.fori_loop(0, n_chunks, body, h)

    def backward(gate_ref, y_ref, off, n_rows, h):
        n_chunks = n_rows // CONV_ROWS

        def body(i, h):
            r = n_chunks - 1 - i
            start = pl.multiple_of(r * CONV_ROWS, CONV_ROWS)
            rows = pl.ds(off + start, CONV_ROWS)
            a_cum, h0 = _lru_scan(ab_s[rows, :], ub_s[rows, :], False)
            hb = h0 + a_cum * h
            y = (hf_s[rows, :] + hb) * jax.nn.gelu(gate_ref[pl.ds(start, CONV_ROWS), :])
            y_ref[pl.ds(start, CONV_ROWS), :] = y.astype(y_ref.dtype)
            return hb[0:1, :]

        return lax.fori_loop(0, n_chunks, body, h)

    zero = jnp.zeros((1, W), F32)
    forward(xl_ref, ctx, seq, forward(xc_ref, 0, ctx, zero))
    backward(gl_ref, yl_ref, ctx, seq, backward(gc_ref, yc_ref, 0, ctx, zero))


def _lru_call(p, conv_w, conv_b, w_blk, b_blk, lam, *, batch, seq, ctx):
    cb0 = batch * seq // ctx
    x_col = (3 * GDN_WIDTH + GDN_WIDTH) // LRU_WIDTH
    T = seq + ctx
    full = lambda a: pl.BlockSpec(a.shape, lambda b: (0,) * a.ndim)
    return pl.pallas_call(
        functools.partial(_lru_kernel, seq=seq, ctx=ctx),
        out_shape=(jax.ShapeDtypeStruct((batch * seq, LRU_WIDTH), BF16),
                   jax.ShapeDtypeStruct((batch * ctx, LRU_WIDTH), BF16)),
        grid=(batch,),
        in_specs=[pl.BlockSpec((seq, LRU_WIDTH), lambda b: (b, x_col)),
                  pl.BlockSpec((ctx, LRU_WIDTH), lambda b: (cb0 + b, x_col)),
                  pl.BlockSpec((seq, LRU_WIDTH), lambda b: (b, x_col + 1)),
                  pl.BlockSpec((ctx, LRU_WIDTH), lambda b: (cb0 + b, x_col + 1)),
                  full(conv_w), full(conv_b), full(w_blk), full(b_blk), full(lam)],
        out_specs=(pl.BlockSpec((seq, LRU_WIDTH), lambda b: (b, 0)),
                   pl.BlockSpec((ctx, LRU_WIDTH), lambda b: (b, 0))),
        scratch_shapes=[pltpu.VMEM((T, LRU_WIDTH), F32)] * 3,
        compiler_params=_params("arbitrary"),
        name="rglru",
    )(p, p, p, p, conv_w, conv_b, w_blk, b_blk, lam)


def _attn_kernel(*refs, n_src):
    q_ref = refs[0]
    srcs = [(refs[1 + 2 * i], refs[2 + 2 * i]) for i in range(n_src)]
    o_ref, s_scr, p_scr = refs[1 + 2 * n_src:]
    tq = q_ref.shape[0]
    n_keys = s_scr.shape[1]
    lane = lax.broadcasted_iota(jnp.int32, (tq, LANES), 1)
    per_vreg = LANES // MLA_V
    outs = []
    for h in range(MLA_HEADS):
        hrows = slice(h * LANES, (h + 1) * LANES)
        qh = q_ref[:, hrows]
        off = 0
        for kt_ref, _ in srcs:
            n = kt_ref.shape[1]
            s_scr[:, off:off + n] = jnp.dot(qh, kt_ref[hrows, :], preferred_element_type=F32)
            off += n

        def col_blocks(fn, carry):
            first = n_keys % ATT_KB
            if first:
                carry = fn(0, first, carry)
            body = lambda j, c: fn(pl.multiple_of(first + j * ATT_KB, LANES), ATT_KB, c)
            return lax.fori_loop(0, n_keys // ATT_KB, body, carry)

        def max_blk(start, w, mp):
            s = s_scr[:, pl.ds(start, w)]
            for i in range(w // LANES):
                mp = jnp.maximum(mp, s[:, i * LANES:(i + 1) * LANES])
            return mp

        m = jnp.max(col_blocks(max_blk, jnp.full((tq, LANES), -jnp.inf, F32)), axis=-1, keepdims=True)

        def exp_blk(start, w, lp):
            p = jnp.exp2(s_scr[:, pl.ds(start, w)] - m)
            p_scr[:, pl.ds(start, w)] = p.astype(BF16)
            for i in range(w // LANES):
                lp = lp + p[:, i * LANES:(i + 1) * LANES]
            return lp

        l_sum = jnp.sum(col_blocks(exp_blk, jnp.zeros((tq, LANES), F32)), axis=-1, keepdims=True)
        vcols = slice((h // per_vreg) * LANES, (h // per_vreg + 1) * LANES)
        acc = None
        off = 0
        for _, v_ref in srcs:
            n = v_ref.shape[0]
            part = jnp.dot(p_scr[:, off:off + n], v_ref[:, vcols], preferred_element_type=F32)
            acc = part if acc is None else acc + part
            off += n
        outs.append(acc * (1.0 / l_sum))
    tiles = []
    for t in range(MLA_HEADS // per_vreg):
        tile = outs[t * per_vreg]
        for j in range(1, per_vreg):
            tile = jnp.where(lane >= j * MLA_V, outs[t * per_vreg + j], tile)
        tiles.append(tile)
    o_ref[...] = jnp.concatenate(tiles, axis=1).astype(o_ref.dtype)


def _attn_call(q, kt, v, *, batch, seq, ctx, latent):
    hw = MLA_HEADS * LANES
    cb0 = batch * seq // ctx
    kv_specs = [pl.BlockSpec((hw, ctx), lambda b, i: (0, cb0 + b)),
                pl.BlockSpec((ctx, MLA_WIDTH), lambda b, i: (cb0 + b, 0))]
    args = (q, kt, v)
    if latent:
        per_b = seq // ATT_TQ
        grid = (batch, per_b)
        q_spec = pl.BlockSpec((ATT_TQ, hw), lambda b, i: (b * per_b + i, 0))
        kv_specs += [pl.BlockSpec((hw, seq), lambda b, i: (0, b)),
                     pl.BlockSpec((seq, MLA_WIDTH), lambda b, i: (b, 0))]
        args += (kt, v)
        out_spec = pl.BlockSpec((ATT_TQ, MLA_WIDTH), lambda b, i: (b * per_b + i, 0))
        rows, tq, keys = batch * seq, ATT_TQ, ctx + seq
    else:
        grid = (batch, 1)
        q_spec = pl.BlockSpec((ctx, hw), lambda b, i: (cb0 + b, 0))
        out_spec = pl.BlockSpec((ctx, MLA_WIDTH), lambda b, i: (b, 0))
        rows, tq, keys = batch * ctx, ctx, ctx
    return pl.pallas_call(
        functools.partial(_attn_kernel, n_src=len(kv_specs) // 2),
        out_shape=jax.ShapeDtypeStruct((rows, MLA_WIDTH), BF16),
        grid=grid,
        in_specs=[q_spec] + kv_specs,
        out_specs=out_spec,
        scratch_shapes=[pltpu.VMEM((tq, keys), F32), pltpu.VMEM((tq, keys), BF16)],
        compiler_params=_params("arbitrary", "arbitrary"),
        name="mla_lat" if latent else "mla_ctx",
    )(*args)


def _outmlp_kernel(*refs, with_ctx, n_lat_blocks):
    x_ref, mod_ref = refs[0], refs[1]
    n_y = 6 if with_ctx else 3
    y_refs = refs[2:2 + n_y]
    gpost_ref, gpre_ref, gmlp_ref, wo_ref, w1_ref, w2_ref, o_ref = refs[2 + n_y:]
    if with_ctx:
        is_lat = pl.program_id(0) < n_lat_blocks
        parts = [jnp.where(is_lat, y_refs[2 * i][...], y_refs[2 * i + 1][...]) for i in range(3)]
    else:
        parts = [r[...] for r in y_refs]
    y = jnp.concatenate(parts, axis=1)
    t = jnp.dot(y, wo_ref[...], preferred_element_type=F32)
    x1 = x_ref[...] + mod_ref[2:3, :] * _rms(t, gpost_ref[...])
    h2 = (_rms(x1, gpre_ref[...]) * (1.0 + mod_ref[4:5, :]) + mod_ref[3:4, :]).astype(BF16)
    acc = jnp.zeros(x1.shape, F32)
    for f in range(D_FF // FF_CHUNK):
        cols = slice(f * FF_CHUNK, (f + 1) * FF_CHUNK)
        a = jnp.dot(h2, w1_ref[:, cols], preferred_element_type=F32)
        a = jnp.square(jnp.maximum(a, 0.0)).astype(BF16)
        acc = acc + jnp.dot(a, w2_ref[cols, :], preferred_element_type=F32)
    o_ref[...] = x1 + mod_ref[5:6, :] * _rms(acc, gmlp_ref[...])


def _outmlp_call(xs, mods_l, ys, gpost, gpre, gmlp, wo, w1, w2, *, batch, seq, with_ctx):
    n = xs.shape[0]
    nlb = batch * seq // TM
    nblk = n // TM if with_ctx else nlb
    per_b = seq // TM
    modrow = lambda i: jnp.where(i < nlb, i // per_b, batch)
    rowblk = lambda w: pl.BlockSpec((TM, w), lambda i: (i, 0))
    resident = lambda a: pl.BlockSpec(a.shape, lambda i: (0,) * a.ndim, pipeline_mode=pl.Buffered(1))
    y_specs, y_args = [], []
    for y_lat, y_ctx in ys:
        w = y_lat.shape[1]
        y_specs.append(pl.BlockSpec((TM, w), lambda i: (jnp.minimum(i, nlb - 1), 0)))
        y_args.append(y_lat)
        if with_ctx:
            y_specs.append(pl.BlockSpec((TM, w), lambda i: (jnp.maximum(i - nlb, 0), 0)))
            y_args.append(y_ctx)
    return pl.pallas_call(
        functools.partial(_outmlp_kernel, with_ctx=with_ctx, n_lat_blocks=nlb),
        out_shape=jax.ShapeDtypeStruct((nblk * TM, D_MODEL), F32),
        grid=(nblk,),
        in_specs=[rowblk(D_MODEL), pl.BlockSpec((None, 6, D_MODEL), lambda i: (modrow(i), 0, 0))] + y_specs
                 + [resident(gpost), resident(gpre), resident(gmlp), resident(wo), resident(w1), resident(w2)],
        out_specs=rowblk(D_MODEL),
        compiler_params=_params("arbitrary"),
        name="outproj_mlp",
    )(xs, mods_l, *y_args, gpost, gpre, gmlp, wo, w1, w2)


def _rope_rot_cols(w):
    w4 = w.reshape(w.shape[:-1] + (2, 2, MLA_ROPE // 4))
    return jnp.stack([-w4[..., 1, :], w4[..., 0, :]], axis=-2).reshape(w.shape)


def _zeros_like_cols(w, n):
    return jnp.zeros(w.shape[:-1] + (n,), w.dtype)


def _layout_w_in(w_in):
    o = np.concatenate([[0], np.cumsum(IN_SIZES)])
    qkv, z, beta, dec, lx, lg, cq, ckv, kr = [w_in[..., o[i]:o[i + 1]] for i in range(len(IN_SIZES))]
    zc = functools.partial(_zeros_like_cols, w_in)
    main = jnp.concatenate([qkv, z, lx, lg], axis=-1)
    pad = LANES - MLA_NOPE - MLA_ROPE
    rest = jnp.concatenate([cq, ckv, beta, dec, zc(LANES - 4 * GDN_HEADS),
                            zc(MLA_NOPE), kr, zc(pad), zc(MLA_NOPE), _rope_rot_cols(kr), zc(pad)], axis=-1)
    return main.astype(BF16), rest.astype(BF16)


def _layout_mla(w_uq, w_ukv):
    lead = w_uq.shape[:-1]
    q4 = w_uq.reshape(lead + (MLA_HEADS, MLA_NOPE + MLA_ROPE))
    nope, rope = q4[..., :MLA_NOPE], q4[..., MLA_NOPE:]
    pad = LANES - MLA_NOPE - MLA_ROPE
    wa = jnp.concatenate([nope, rope, _zeros_like_cols(rope, pad)], axis=-1)
    wb = jnp.concatenate([jnp.zeros_like(nope), _rope_rot_cols(rope), _zeros_like_cols(rope, pad)], axis=-1)
    wq = jnp.concatenate([wa.reshape(lead + (-1,)), wb.reshape(lead + (-1,))], axis=-1)
    lead = w_ukv.shape[:-1]
    kv4 = w_ukv.reshape(lead + (MLA_HEADS, MLA_NOPE + MLA_V))
    k_nope, v = kv4[..., :MLA_NOPE], kv4[..., MLA_NOPE:]
    wkn = jnp.concatenate([k_nope, _zeros_like_cols(k_nope, LANES - MLA_NOPE)], axis=-1).reshape(lead + (-1,))
    return wq.astype(BF16), wkn.astype(BF16), v.reshape(lead + (-1,)).astype(BF16)


def _rope_table(seq):
    rows = seq // GRID_W
    row = jnp.repeat(jnp.arange(rows, dtype=F32), GRID_W)
    col = jnp.tile(jnp.arange(GRID_W, dtype=F32), rows)
    half = MLA_ROPE // 2
    inv = ROPE_BASE ** (-jnp.arange(0, half, 2, dtype=F32) / half)
    ang = jnp.stack([row[:, None] * inv, col[:, None] * inv], axis=1)
    ang = jnp.concatenate([ang, ang], axis=-1).reshape(seq, MLA_ROPE)
    cos = jnp.concatenate([jnp.cos(ang), jnp.ones((TM, MLA_ROPE), F32)], axis=0)
    sin = jnp.concatenate([jnp.sin(ang), jnp.zeros((TM, MLA_ROPE), F32)], axis=0)
    n = seq + TM
    one, zero = jnp.ones((n, MLA_NOPE), F32), jnp.zeros((n, MLA_NOPE), F32)
    pad = jnp.zeros((n, LANES - MLA_NOPE - MLA_ROPE), F32)
    scale = (MLA_NOPE + MLA_ROPE) ** -0.5 * math.log2(math.e)
    return jnp.concatenate([scale * one, scale * cos, pad, zero, scale * sin, pad,
                            zero, cos, pad, zero, sin, pad], axis=1)


def _layout_lru(w_a, b_a, w_i, b_i):
    def blockdiag(w):
        g, n = w.shape[1], w.shape[2]
        out = jnp.zeros((w.shape[0], g * n, g * n), w.dtype)
        for j in range(g):
            out = out.at[:, j * n:(j + 1) * n, j * n:(j + 1) * n].set(w[:, j])
        return out

    da, di = blockdiag(w_a), blockdiag(w_i)
    w = jnp.concatenate([da[0], di[0], da[1], di[1]], axis=-1).astype(BF16)
    b = jnp.concatenate([b_a[0], b_i[0], b_a[1], b_i[1]], axis=-1)[None, :]
    return w, b


def kernel(x, c, ctx, c_ctx, w_ada, b_ada, g_attn_pre, g_attn_post, g_mlp_pre, g_mlp_post, w_in, gdn_conv_w,
           gdn_a_log, gdn_dt_bias, gdn_norm_w, lru_conv_w, lru_conv_b, lru_w_a, lru_b_a, lru_w_i, lru_b_i,
           lru_lambda, mla_q_norm, mla_w_uq, mla_kv_norm, mla_w_ukv, w_out, w_mlp1, w_mlp2):
    batch, seq, _ = x.shape
    n_ctx = ctx.shape[1]
    depth = w_ada.shape[0]
    assert seq % TM == 0 and (batch * n_ctx) % TM == 0 and n_ctx % CONV_ROWS == 0 and seq % GRID_W == 0
    assert (batch * seq) % n_ctx == 0 and batch + 1 <= 2 * SUBLANES
    dims = dict(batch=batch, seq=seq, ctx=n_ctx)

    cond = jnp.concatenate([c, c_ctx[None, :], jnp.zeros((2 * SUBLANES - batch - 1, D_MODEL), F32)], axis=0)
    mods = _ada_call(cond, w_ada, b_ada).reshape(depth, 2 * SUBLANES, 6, D_MODEL)
    tab = _rope_table(seq)
    xs = jnp.concatenate([x.reshape(batch * seq, D_MODEL), ctx.reshape(batch * n_ctx, D_MODEL)], axis=0)
    row = lambda a: a[None, :]

    for l in range(depth):
        last = l == depth - 1
        wm, wr = _layout_w_in(w_in[l])
        wq, wkn, wv = _layout_mla(mla_w_uq[l], mla_w_ukv[l])
        p, bg, q, kt, v = _inproj_call(xs, mods[l], row(g_attn_pre[l]), wm, wr, row(mla_q_norm[l]), wq,
                                      row(mla_kv_norm[l]), wkn, wv, tab, batch=batch, seq=seq)
        pad = jnp.zeros((LANES - 4 * GDN_HEADS,), F32)
        ald = jnp.stack([jnp.concatenate([jnp.zeros((2 * GDN_HEADS,), F32), gdn_a_log[l].reshape(-1), pad]),
                         jnp.concatenate([jnp.zeros((2 * GDN_HEADS,), F32), gdn_dt_bias[l].reshape(-1), pad])])
        y_gdn = _gdn_call(p, bg, gdn_conv_w[l], ald, row(gdn_norm_w[l]), **dims)
        w_blk, b_blk = _layout_lru(lru_w_a[l], lru_b_a[l], lru_w_i[l], lru_b_i[l])
        y_lru = _lru_call(p, lru_conv_w[l], row(lru_conv_b[l]), w_blk, b_blk, lru_lambda[l], **dims)
        y_att = (_attn_call(q, kt, v, latent=True, **dims),
                 None if last else _attn_call(q, kt, v, latent=False, **dims))
        xs = _outmlp_call(xs, mods[l], (y_gdn, y_lru, y_att), row(g_attn_post[l]), row(g_mlp_pre[l]),
                          row(g_mlp_post[l]), w_out[l].astype(BF16), w_mlp1[l].astype(BF16),
                          w_mlp2[l].astype(BF16), batch=batch, seq=seq, with_ctx=not last)
    return xs.reshape(batch, seq, D_MODEL)
```

```python
import functools
import math

import numpy as np
import jax
import jax.numpy as jnp
from jax import lax
from jax.experimental import pallas as pl
from jax.experimental.pallas import tpu as pltpu

F32 = jnp.float32
BF16 = jnp.bfloat16

D_MODEL = 1024
EPS = 1e-6
GRID_W = 64
CONV_WIDTH = 4
GDN_HEAD_DIM = 128
GDN_HEADS = 4
GDN_WIDTH = GDN_HEADS * GDN_HEAD_DIM
GDN_CHUNK = 64
LRU_WIDTH = 256
LRU_BLOCKS = 4
LRU_C = 8.0
MLA_V = 64
MLA_HEADS = 4
MLA_WIDTH = MLA_HEADS * MLA_V
MLA_NOPE = 64
MLA_ROPE = 32
MLA_Q_RANK = 256
MLA_KV_RANK = 128
ROPE_BASE = 10000.0
D_FF = 4 * D_MODEL
IN_SIZES = (3 * GDN_WIDTH, GDN_WIDTH, 2 * GDN_HEADS, 2 * GDN_HEADS, LRU_WIDTH, LRU_WIDTH,
            MLA_Q_RANK, MLA_KV_RANK, MLA_ROPE)

LANES = 128
SUBLANES = 8
TM = 512
MAIN_COLS = 3 * GDN_WIDTH + GDN_WIDTH + 2 * LRU_WIDTH
REST_COLS = MLA_Q_RANK + MLA_KV_RANK + 3 * LANES
CONV_ROWS = 256
ATT_TQ = 256
ATT_KB = 512
FF_CHUNK = 1024
VMEM_LIMIT = 56 * 1024 * 1024

_NT = (((1,), (1,)), ((), ()))
_TN = (((0,), (0,)), ((), ()))


def _rms(x, g):
    return x * lax.rsqrt(jnp.mean(x * x, axis=-1, keepdims=True) + EPS) * g


def _bdot(a, b):
    return jnp.dot(a.astype(BF16), b.astype(BF16), preferred_element_type=F32)


def _params(*sem):
    return pltpu.CompilerParams(dimension_semantics=sem, vmem_limit_bytes=VMEM_LIMIT)


def _ada_kernel(c_ref, w_ref, b_ref, o_ref):
    o_ref[...] = _bdot(jax.nn.silu(c_ref[...]), w_ref[...]) + b_ref[...]


def _ada_call(cond, w_ada, b_ada):
    depth = w_ada.shape[0]
    rows = cond.shape[0]
    return pl.pallas_call(
        _ada_kernel,
        out_shape=jax.ShapeDtypeStruct((depth, rows, 6 * D_MODEL), F32),
        grid=(depth, 6),
        in_specs=[pl.BlockSpec((rows, D_MODEL), lambda l, j: (0, 0)),
                  pl.BlockSpec((None, D_MODEL, D_MODEL), lambda l, j: (l, 0, j)),
                  pl.BlockSpec((None, 1, D_MODEL), lambda l, j: (l, 0, j))],
        out_specs=pl.BlockSpec((None, rows, D_MODEL), lambda l, j: (l, 0, j)),
        compiler_params=_params("arbitrary", "arbitrary"),
        name="adaln",
    )(cond, w_ada, b_ada.reshape(depth, 1, 6 * D_MODEL))


def _gdn_gates(raw, a_log, dt_b):
    rows = raw.shape[0]
    lane = lax.broadcasted_iota(jnp.int32, raw.shape, 1)
    pos = lax.broadcasted_iota(jnp.int32, raw.shape, 0) % GDN_CHUNK
    g = -jnp.exp(a_log) * jax.nn.softplus(raw + dt_b)
    cum_f, cum_b = g, g
    s = 1
    while s < GDN_CHUNK:
        cum_f = cum_f + jnp.where(pos >= s, pltpu.roll(cum_f, s, axis=0), 0.0)
        cum_b = cum_b + jnp.where(pos < GDN_CHUNK - s, pltpu.roll(cum_b, rows - s, axis=0), 0.0)
        s *= 2
    return jnp.where(lane < 2 * GDN_HEADS, jax.nn.sigmoid(raw), jnp.where(lane < 3 * GDN_HEADS, cum_f, cum_b))


def _inproj_kernel(x_ref, mod_ref, g_ref, wm_ref, wr_ref, qn_ref, wq_ref, kvn_ref, wkn_ref, wv_ref, tab_ref, ald_ref,
                   p_ref, gg_ref, q_ref, kt_ref, v_ref):
    h = _rms(x_ref[...], g_ref[...]) * (1.0 + mod_ref[1:2, :]) + mod_ref[0:1, :]
    hb = h.astype(BF16)
    p_ref[...] = jnp.dot(hb, wm_ref[...], preferred_element_type=F32)
    r = jnp.dot(hb, wr_ref[...], preferred_element_type=F32)
    o0 = MLA_Q_RANK
    o1 = o0 + MLA_KV_RANK
    cq = r[:, :o0]
    ckv = r[:, o0:o1]
    gg_ref[...] = _gdn_gates(r[:, o1:o1 + LANES], ald_ref[0:1, :], ald_ref[1:2, :])
    kr_a = r[:, o1 + LANES:o1 + 2 * LANES]
    kr_b = r[:, o1 + 2 * LANES:o1 + 3 * LANES]
    tab = tab_ref[...]
    tile = lambda t: jnp.concatenate([t] * MLA_HEADS, axis=1)
    q2 = _bdot(_rms(cq, qn_ref[...]), wq_ref[...])
    hw = MLA_HEADS * LANES
    q_ref[...] = (q2[:, :hw] * tile(tab[:, :LANES]) + q2[:, hw:] * tile(tab[:, LANES:2 * LANES])).astype(BF16)
    ckvn = _rms(ckv, kvn_ref[...]).astype(BF16)
    kr = kr_a * tab[:, 2 * LANES:3 * LANES] + kr_b * tab[:, 3 * LANES:]
    kt_ref[...] = (jnp.dot(ckvn, wkn_ref[...], preferred_element_type=F32) + tile(kr)).T.astype(BF16)
    v_ref[...] = jnp.dot(ckvn, wv_ref[...], preferred_element_type=F32).astype(BF16)


def _inproj_call(xs, mods_l, g, wm, wr, qn, wq, kvn, wkn, wv, tab, ald, *, batch, seq):
    n = xs.shape[0]
    nblk = n // TM
    nlb = batch * seq // TM
    per_b = seq // TM
    modrow = lambda i: jnp.where(i < nlb, i // per_b, batch)
    tabrow = lambda i: jnp.where(i < nlb, i % per_b, per_b)
    full = lambda a: pl.BlockSpec(a.shape, lambda i: (0,) * a.ndim)
    rowblk = lambda w: pl.BlockSpec((TM, w), lambda i: (i, 0))
    hw = MLA_HEADS * LANES
    return pl.pallas_call(
        _inproj_kernel,
        out_shape=(jax.ShapeDtypeStruct((n, MAIN_COLS), F32),
                   jax.ShapeDtypeStruct((n, LANES), F32),
                   jax.ShapeDtypeStruct((n, hw), BF16),
                   jax.ShapeDtypeStruct((hw, n), BF16),
                   jax.ShapeDtypeStruct((n, MLA_WIDTH), BF16)),
        grid=(nblk,),
        in_specs=[rowblk(D_MODEL),
                  pl.BlockSpec((None, 6, D_MODEL), lambda i: (modrow(i), 0, 0)),
                  full(g), full(wm), full(wr), full(qn), full(wq), full(kvn), full(wkn), full(wv),
                  pl.BlockSpec((TM, 4 * LANES), lambda i: (tabrow(i), 0)), full(ald)],
        out_specs=(rowblk(MAIN_COLS), rowblk(LANES), rowblk(hw), pl.BlockSpec((hw, TM), lambda i: (0, i)),
                   rowblk(MLA_WIDTH)),
        compiler_params=_params("arbitrary"),
        name="inproj",
    )(xs, mods_l, g, wm, wr, qn, wq, kvn, wkn, wv, tab, ald)


def _conv_chunk(src_ref, r, n_chunks, n_rows, w, rows=CONV_ROWS):
    start = pl.multiple_of(r * rows, rows)
    cur = src_ref[pl.ds(start, rows), :]
    prev_start = pl.multiple_of(jnp.maximum(start - SUBLANES, 0), SUBLANES)
    prev = jnp.where(r > 0, src_ref[pl.ds(prev_start, SUBLANES), :], 0.0)
    next_start = pl.multiple_of(jnp.minimum(start + rows, n_rows - SUBLANES), SUBLANES)
    nxt = jnp.where(r < n_chunks - 1, src_ref[pl.ds(next_start, SUBLANES), :], 0.0)
    cat = jnp.concatenate([prev, cur, nxt], axis=0)
    tot = rows + 2 * SUBLANES
    sl = slice(SUBLANES, SUBLANES + rows)
    acc = w[2:3, :] * cur
    acc = acc + w[0:1, :] * pltpu.roll(cat, 2, axis=0)[sl]
    acc = acc + w[1:2, :] * pltpu.roll(cat, 1, axis=0)[sl]
    acc = acc + w[3:4, :] * pltpu.roll(cat, tot - 1, axis=0)[sl]
    return start, acc


GDN_UNROLL = 8


def _gdn_kernel(ql_ref, kl_ref, vl_ref, qc_ref, kc_ref, vc_ref, zl_ref, zc_ref, ggl_ref, ggc_ref,
                cwq_ref, cwk_ref, cwv_ref, nw_ref, yl_ref, yc_ref,
                lhs_s, upd_s, gl_s, oacc, *, seq, ctx):
    head = pl.program_id(1)
    C = GDN_CHUNK
    C2 = 2 * C
    hd = GDN_HEAD_DIM
    n_ctx = ctx // C
    n_tot = (ctx + seq) // C

    lane = lax.broadcasted_iota(jnp.int32, (C, LANES), 1)
    sub_t = lax.broadcasted_iota(jnp.int32, (LANES, C), 0)
    prow = lax.broadcasted_iota(jnp.int32, (C2, C2), 0)
    pcol = lax.broadcasted_iota(jnp.int32, (C2, C2), 1)
    sgn = 1 - 2 * (prow // C)
    incl_p = ((prow // C) == (pcol // C)) & ((prow - pcol) * sgn >= 0)
    eye_p = (prow == pcol).astype(F32)
    off_diag = []
    g = 1
    while g < C:
        half = (prow % (2 * g)) // g - (pcol % (2 * g)) // g
        off_diag.append(((prow // (2 * g)) == (pcol // (2 * g))) & (half * sgn == 1))
        g *= 2
    cws = (cwq_ref[...], cwk_ref[...], cwv_ref[...])
    dup_rows = lambda t: jnp.concatenate([t, t], axis=0)

    def local_chunks(srcs, gg_ref, rs, n_chunks, n_rows, c_off):
        U = range(len(rs))
        q, k, v, gg = [], [], [], []
        for r in rs:
            qkv = []
            for (src, w, kind) in zip(srcs, cws, "qkv"):
                start, acc = _conv_chunk(src, r, n_chunks, n_rows, w, rows=C)
                y = jax.nn.silu(acc)
                if kind != "v":
                    y = y * lax.rsqrt(jnp.sum(y * y, axis=-1, keepdims=True) + EPS)
                if kind == "q":
                    y = y * (hd ** -0.5)
                qkv.append(y)
            q.append(qkv[0]); k.append(qkv[1]); v.append(qkv[2])
            gg.append(gg_ref[pl.ds(start, C), :])
        r_qk = [lax.dot_general(jnp.concatenate([q[u], k[u]], axis=0).astype(BF16), dup_rows(k[u].astype(BF16)),
                                _NT, preferred_element_type=F32) for u in U]
        g_col, b_col, g_last, decay = [], [], [], []
        for u in U:
            g_cols, b_cols, g_rows, g_lasts = [], [], [], []
            gg_t = gg[u].T
            for d in range(2):
                g_lane = 2 * GDN_HEADS + d * GDN_HEADS + head
                gc = jnp.sum(jnp.where(lane == g_lane, gg[u], 0.0), axis=-1, keepdims=True)
                g_cols.append(gc)
                b_cols.append(jnp.sum(jnp.where(lane == d * GDN_HEADS + head, gg[u], 0.0), axis=-1,
                                      keepdims=True))
                g_rows.append(jnp.sum(jnp.where(sub_t == g_lane, gg_t, 0.0), axis=0, keepdims=True))
                g_lasts.append(gc[C - 1:C, :] if d == 0 else gc[0:1, :])
            g_col.append(jnp.concatenate(g_cols, axis=0))
            b_col.append(jnp.concatenate(b_cols, axis=0))
            g_last.append(g_lasts)
            g_row = jnp.concatenate(g_rows, axis=1)
            decay.append(jnp.exp(jnp.where(incl_p, g_col[u] - g_row, -jnp.inf)))
        attn = [dup_rows(r_qk[u][:C]) * decay[u] for u in U]
        low = [dup_rows(r_qk[u][C:]) * b_col[u] * decay[u] for u in U]
        t_inv = [eye_p - jnp.where(off_diag[0], low[u], 0.0) for u in U]
        for m in off_diag[1:]:
            xt = [_bdot(jnp.where(m, low[u], 0.0), t_inv[u]) for u in U]
            t_inv = [t_inv[u] - _bdot(t_inv[u], xt[u]) for u in U]
        e_g = [jnp.exp(g_col[u]) for u in U]
        x = [_bdot(t_inv[u], jnp.concatenate([dup_rows(k[u]) * (e_g[u] * b_col[u]), dup_rows(v[u]) * b_col[u]],
                                             axis=1)) for u in U]
        ax = [_bdot(attn[u], x[u]) for u in U]
        zero = jnp.zeros((C, hd), F32)
        kx = []
        for u in U:
            gl_col = jnp.concatenate([jnp.broadcast_to(g_last[u][0], (C, 1)),
                                      jnp.broadcast_to(g_last[u][1], (C, 1))], axis=0)
            kg = dup_rows(k[u]) * jnp.exp(gl_col - g_col[u])
            kg_blk = jnp.concatenate([jnp.concatenate([kg[:C], zero], axis=1),
                                      jnp.concatenate([zero, kg[C:]], axis=1)], axis=0)
            kx.append(lax.dot_general(kg_blk.astype(BF16), x[u].astype(BF16), _TN,
                                      preferred_element_type=F32))
        for u in U:
            c_glob = c_off + rs[u]
            q_eff = dup_rows(q[u]) * e_g[u] - ax[u][:, :hd]
            for d in range(2):
                lhs_s[d, c_glob, 0:C, :] = q_eff[d * C:(d + 1) * C].astype(BF16)
                lhs_s[d, c_glob, C:C + hd, :] = kx[u][d * hd:(d + 1) * hd, :hd].astype(BF16)
                upd_s[d, c_glob] = kx[u][d * hd:(d + 1) * hd, hd:]
                gl_s[d, c_glob] = jnp.broadcast_to(jnp.exp(g_last[u][d]), (SUBLANES, LANES))
            oacc[pl.ds(pl.multiple_of(c_glob * C, C), C), :] = ax[u][:C, hd:] + ax[u][C:, hd:]

    def local_segment(srcs, gg_ref, n_rows, c_off):
        n_chunks = n_rows // C
        unroll = min(GDN_UNROLL, n_chunks)

        def body(i, carry):
            local_chunks(srcs, gg_ref, [i * unroll + u for u in range(unroll)], n_chunks, n_rows, c_off)
            return carry

        lax.fori_loop(0, n_chunks // unroll, body, 0)

    local_segment((qc_ref, kc_ref, vc_ref), ggc_ref, ctx, 0)
    local_segment((ql_ref, kl_ref, vl_ref), ggl_ref, seq, n_ctx)

    def step(s, carry):
        chunks = (s, jnp.where(s < n_ctx, n_ctx - 1 - s, n_tot - 1 - (s - n_ctx)))
        new = []
        for d in range(2):
            c = chunks[d]
            res = jnp.dot(lhs_s[d, c], carry[d].astype(BF16), preferred_element_type=F32)
            rows = pl.ds(pl.multiple_of(c * C, C), C)
            oacc[rows, :] = oacc[rows, :] + res[:C]
            new.append(carry[d] * gl_s[d, c, 0:1, :] + upd_s[d, c] - res[C:])
        return tuple(new)

    zero = jnp.zeros((hd, hd), F32)
    lax.fori_loop(0, n_tot, step, (zero, zero))

    def finish(z_ref, y_ref, off, n_rows):
        def body(r, carry):
            start = pl.multiple_of(r * CONV_ROWS, CONV_ROWS)
            o = oacc[pl.ds(off + start, CONV_ROWS), :]
            y = _rms(o, nw_ref[...]) * jax.nn.silu(z_ref[pl.ds(start, CONV_ROWS), :])
            y_ref[pl.ds(start, CONV_ROWS), :] = y.astype(y_ref.dtype)
            return carry

        lax.fori_loop(0, n_rows // CONV_ROWS, body, 0)

    finish(zc_ref, yc_ref, 0, ctx)
    finish(zl_ref, yl_ref, ctx, seq)


def _gdn_call(p, gg, conv_w, norm_w, *, batch, seq, ctx):
    H = GDN_HEADS
    C = GDN_CHUNK
    cb0 = batch * seq // ctx
    lat = lambda cblk: pl.BlockSpec((seq, LANES), lambda b, h: (b, cblk * H + h))
    cx = lambda cblk: pl.BlockSpec((ctx, LANES), lambda b, h: (cb0 + b, cblk * H + h))
    cw = lambda cblk: pl.BlockSpec((CONV_WIDTH, LANES), lambda b, h: (0, cblk * H + h))
    n_tot = (seq + ctx) // C
    assert all((n // C) % min(GDN_UNROLL, n // C) == 0 for n in (seq, ctx))
    return pl.pallas_call(
        functools.partial(_gdn_kernel, seq=seq, ctx=ctx),
        out_shape=(jax.ShapeDtypeStruct((batch * seq, GDN_WIDTH), BF16),
                   jax.ShapeDtypeStruct((batch * ctx, GDN_WIDTH), BF16)),
        grid=(batch, H),
        in_specs=[lat(0), lat(1), lat(2), cx(0), cx(1), cx(2), lat(3), cx(3),
                  pl.BlockSpec((seq, LANES), lambda b, h: (b, 0)),
                  pl.BlockSpec((ctx, LANES), lambda b, h: (cb0 + b, 0)),
                  cw(0), cw(1), cw(2),
                  pl.BlockSpec((1, LANES), lambda b, h: (0, 0))],
        out_specs=(pl.BlockSpec((seq, LANES), lambda b, h: (b, h)),
                   pl.BlockSpec((ctx, LANES), lambda b, h: (b, h))),
        scratch_shapes=[pltpu.VMEM((2, n_tot, C + GDN_HEAD_DIM, LANES), BF16),
                        pltpu.VMEM((2, n_tot, GDN_HEAD_DIM, LANES), F32),
                        pltpu.VMEM((2, n_tot, SUBLANES, LANES), F32),
                        pltpu.VMEM((seq + ctx, LANES), F32)],
        compiler_params=_params("arbitrary", "arbitrary"),
        name="gdn",
    )(p, p, p, p, p, p, p, p, gg, gg, conv_w, conv_w, conv_w, norm_w)


def _lru_scan(a, u, fwd):
    n = a.shape[0]
    row = lax.broadcasted_iota(jnp.int32, a.shape, 0)
    s = 1
    while s < n:
        shift = s if fwd else n - s
        valid = (row >= s) if fwd else (row < n - s)
        u = jnp.where(valid, a * pltpu.roll(u, shift, axis=0) + u, u)
        a = jnp.where(valid, a * pltpu.roll(a, shift, axis=0), a)
        s *= 2
    return a, u


def _lru_kernel(xl_ref, xc_ref, gl_ref, gc_ref, cw_ref, cb_ref, w_ref, b_ref, lam_ref, yl_ref, yc_ref,
                ab_s, ub_s, hf_s, *, seq, ctx):
    W = LRU_WIDTH
    neg_log_base = jax.nn.softplus(-lam_ref[...])
    cw = cw_ref[...]

    def forward(src_ref, off, n_rows, h):
        n_chunks = n_rows // CONV_ROWS

        def body(r, h):
            start, acc = _conv_chunk(src_ref, r, n_chunks, n_rows, cw)
            xc = acc + cb_ref[...]
            gates = jax.nn.sigmoid(_bdot(xc, w_ref[...]) + b_ref[...])
            coef = []
            for d in range(2):
                rg = gates[:, 2 * d * W:(2 * d + 1) * W]
                ig = gates[:, (2 * d + 1) * W:(2 * d + 2) * W]
                log_a = -LRU_C * rg * neg_log_base[d:d + 1, :]
                a = jnp.exp(log_a)
                coef.append((a, jnp.sqrt(-jnp.tanh(log_a) * (a * a + 1.0)) * (ig * xc)))
            rows = pl.ds(off + start, CONV_ROWS)
            ab_s[rows, :] = coef[1][0]
            ub_s[rows, :] = coef[1][1]
            a_cum, h0 = _lru_scan(coef[0][0], coef[0][1], True)
            hf = h0 + a_cum * h
            hf_s[rows, :] = hf
            return hf[CONV_ROWS - 1:CONV_ROWS, :]

        return lax.fori_loop(0, n_chunks, body, h)

    def backward(gate_ref, y_ref, off, n_rows, h):
        n_chunks = n_rows // CONV_ROWS

        def body(i, h):
            r = n_chunks - 1 - i
            start = pl.multiple_of(r * CONV_ROWS, CONV_ROWS)
            rows = pl.ds(off + start, CONV_ROWS)
            a_cum, h0 = _lru_scan(ab_s[rows, :], ub_s[rows, :], False)
            hb = h0 + a_cum * h
            y = (hf_s[rows, :] + hb) * jax.nn.gelu(gate_ref[pl.ds(start, CONV_ROWS), :])
            y_ref[pl.ds(start, CONV_ROWS), :] = y.astype(y_ref.dtype)
            return hb[0:1, :]

        return lax.fori_loop(0, n_chunks, body, h)

    zero = jnp.zeros((1, W), F32)
    forward(xl_ref, ctx, seq, forward(xc_ref, 0, ctx, zero))
    backward(gl_ref, yl_ref, ctx, seq, backward(gc_ref, yc_ref, 0, ctx, zero))


def _lru_call(p, conv_w, conv_b, w_blk, b_blk, lam, *, batch, seq, ctx):
    cb0 = batch * seq // ctx
    x_col = (3 * GDN_WIDTH + GDN_WIDTH) // LRU_WIDTH
    T = seq + ctx
    full = lambda a: pl.BlockSpec(a.shape, lambda b: (0,) * a.ndim)
    return pl.pallas_call(
        functools.partial(_lru_kernel, seq=seq, ctx=ctx),
        out_shape=(jax.ShapeDtypeStruct((batch * seq, LRU_WIDTH), BF16),
                   jax.ShapeDtypeStruct((batch * ctx, LRU_WIDTH), BF16)),
        grid=(batch,),
        in_specs=[pl.BlockSpec((seq, LRU_WIDTH), lambda b: (b, x_col)),
                  pl.BlockSpec((ctx, LRU_WIDTH), lambda b: (cb0 + b, x_col)),
                  pl.BlockSpec((seq, LRU_WIDTH), lambda b: (b, x_col + 1)),
                  pl.BlockSpec((ctx, LRU_WIDTH), lambda b: (cb0 + b, x_col + 1)),
                  full(conv_w), full(conv_b), full(w_blk), full(b_blk), full(lam)],
        out_specs=(pl.BlockSpec((seq, LRU_WIDTH), lambda b: (b, 0)),
                   pl.BlockSpec((ctx, LRU_WIDTH), lambda b: (b, 0))),
        scratch_shapes=[pltpu.VMEM((T, LRU_WIDTH), F32)] * 3,
        compiler_params=_params("arbitrary"),
        name="rglru",
    )(p, p, p, p, conv_w, conv_b, w_blk, b_blk, lam)


def _attn_kernel(*refs, n_src):
    q_ref = refs[0]
    srcs = [(refs[1 + 2 * i], refs[2 + 2 * i]) for i in range(n_src)]
    o_ref, s_scr, p_scr = refs[1 + 2 * n_src:]
    tq = q_ref.shape[0]
    lane = lax.broadcasted_iota(jnp.int32, (tq, LANES), 1)
    per_vreg = LANES // MLA_V
    col_blocks = []
    off = 0
    for i, (kt_ref, _) in enumerate(srcs):
        n = kt_ref.shape[1]
        for st in range(0, n, ATT_KB):
            col_blocks.append((i, st, off + st, min(ATT_KB, n - st)))
        off += n

    def fold(acc, x, op):
        for i in range(x.shape[1] // LANES):
            acc = op(acc, x[:, i * LANES:(i + 1) * LANES])
        return acc

    def scores(h):
        hrows = slice(h * LANES, (h + 1) * LANES)
        qh = q_ref[:, hrows]
        m_part = jnp.full((tq, LANES), -jnp.inf, F32)
        for i, st, dst, w in col_blocks:
            s = jnp.dot(qh, srcs[i][0][hrows, st:st + w], preferred_element_type=F32)
            s_scr[h % 2, :, dst:dst + w] = s
            m_part = fold(m_part, s, jnp.maximum)
        return jnp.max(m_part, axis=-1, keepdims=True)

    def probs(h, m):
        l_part = jnp.zeros((tq, LANES), F32)
        for _, _, dst, w in col_blocks:
            p = jnp.exp2(s_scr[h % 2, :, dst:dst + w] - m)
            p_scr[h % 2, :, dst:dst + w] = p.astype(BF16)
            l_part = fold(l_part, p, jnp.add)
        return jnp.sum(l_part, axis=-1, keepdims=True)

    def weighted(h, l_sum):
        vcols = slice((h // per_vreg) * LANES, (h // per_vreg + 1) * LANES)
        acc = None
        off = 0
        for _, v_ref in srcs:
            n = v_ref.shape[0]
            part = jnp.dot(p_scr[h % 2, :, off:off + n], v_ref[:, vcols], preferred_element_type=F32)
            acc = part if acc is None else acc + part
            off += n
        return acc * (1.0 / l_sum)

    outs = []
    m_next = scores(0)
    for h in range(MLA_HEADS):
        m = m_next
        if h + 1 < MLA_HEADS:
            m_next = scores(h + 1)
        outs.append(weighted(h, probs(h, m)))
    tiles = []
    for t in range(MLA_HEADS // per_vreg):
        tile = outs[t * per_vreg]
        for j in range(1, per_vreg):
            tile = jnp.where(lane >= j * MLA_V, outs[t * per_vreg + j], tile)
        tiles.append(tile)
    o_ref[...] = jnp.concatenate(tiles, axis=1).astype(o_ref.dtype)


def _attn_call(q, kt, v, *, batch, seq, ctx, latent):
    hw = MLA_HEADS * LANES
    cb0 = batch * seq // ctx
    kv_specs = [pl.BlockSpec((hw, ctx), lambda b, i: (0, cb0 + b)),
                pl.BlockSpec((ctx, MLA_WIDTH), lambda b, i: (cb0 + b, 0))]
    args = (q, kt, v)
    if latent:
        per_b = seq // ATT_TQ
        grid = (batch, per_b)
        q_spec = pl.BlockSpec((ATT_TQ, hw), lambda b, i: (b * per_b + i, 0))
        kv_specs += [pl.BlockSpec((hw, seq), lambda b, i: (0, b)),
                     pl.BlockSpec((seq, MLA_WIDTH), lambda b, i: (b, 0))]
        args += (kt, v)
        out_spec = pl.BlockSpec((ATT_TQ, MLA_WIDTH), lambda b, i: (b * per_b + i, 0))
        rows, tq, keys = batch * seq, ATT_TQ, ctx + seq
    else:
        grid = (batch, 1)
        q_spec = pl.BlockSpec((ctx, hw), lambda b, i: (cb0 + b, 0))
        out_spec = pl.BlockSpec((ctx, MLA_WIDTH), lambda b, i: (b, 0))
        rows, tq, keys = batch * ctx, ctx, ctx
    return pl.pallas_call(
        functools.partial(_attn_kernel, n_src=len(kv_specs) // 2),
        out_shape=jax.ShapeDtypeStruct((rows, MLA_WIDTH), BF16),
        grid=grid,
        in_specs=[q_spec] + kv_specs,
        out_specs=out_spec,
        scratch_shapes=[pltpu.VMEM((2, tq, keys), F32), pltpu.VMEM((2, tq, keys), BF16)],
        compiler_params=_params("arbitrary", "arbitrary"),
        name="mla_lat" if latent else "mla_ctx",
    )(*args)


def _outmlp_kernel(*refs, with_ctx, n_lat_blocks):
    x_ref, mod_ref = refs[0], refs[1]
    n_y = 6 if with_ctx else 3
    y_refs = refs[2:2 + n_y]
    gpost_ref, gpre_ref, gmlp_ref, wo_ref, w1_ref, w2_ref, o_ref = refs[2 + n_y:]
    if with_ctx:
        is_lat = pl.program_id(0) < n_lat_blocks
        parts = [jnp.where(is_lat, y_refs[2 * i][...], y_refs[2 * i + 1][...]) for i in range(3)]
    else:
        parts = [r[...] for r in y_refs]
    y = jnp.concatenate(parts, axis=1)
    t = jnp.dot(y, wo_ref[...], preferred_element_type=F32)
    x1 = x_ref[...] + mod_ref[2:3, :] * _rms(t, gpost_ref[...])
    h2 = (_rms(x1, gpre_ref[...]) * (1.0 + mod_ref[4:5, :]) + mod_ref[3:4, :]).astype(BF16)
    acc = jnp.zeros(x1.shape, F32)
    for f in range(D_FF // FF_CHUNK):
        cols = slice(f * FF_CHUNK, (f + 1) * FF_CHUNK)
        a = jnp.dot(h2, w1_ref[:, cols], preferred_element_type=F32)
        a = jnp.square(jnp.maximum(a, 0.0)).astype(BF16)
        acc = acc + jnp.dot(a, w2_ref[cols, :], preferred_element_type=F32)
    o_ref[...] = x1 + mod_ref[5:6, :] * _rms(acc, gmlp_ref[...])


def _outmlp_call(xs, mods_l, ys, gpost, gpre, gmlp, wo, w1, w2, *, batch, seq, with_ctx):
    n = xs.shape[0]
    nlb = batch * seq // TM
    nblk = n // TM if with_ctx else nlb
    per_b = seq // TM
    modrow = lambda i: jnp.where(i < nlb, i // per_b, batch)
    rowblk = lambda w: pl.BlockSpec((TM, w), lambda i: (i, 0))
    resident = lambda a: pl.BlockSpec(a.shape, lambda i: (0,) * a.ndim, pipeline_mode=pl.Buffered(1))
    y_specs, y_args = [], []
    for y_lat, y_ctx in ys:
        w = y_lat.shape[1]
        y_specs.append(pl.BlockSpec((TM, w), lambda i: (jnp.minimum(i, nlb - 1), 0)))
        y_args.append(y_lat)
        if with_ctx:
            y_specs.append(pl.BlockSpec((TM, w), lambda i: (jnp.maximum(i - nlb, 0), 0)))
            y_args.append(y_ctx)
    return pl.pallas_call(
        functools.partial(_outmlp_kernel, with_ctx=with_ctx, n_lat_blocks=nlb),
        out_shape=jax.ShapeDtypeStruct((nblk * TM, D_MODEL), F32),
        grid=(nblk,),
        in_specs=[rowblk(D_MODEL), pl.BlockSpec((None, 6, D_MODEL), lambda i: (modrow(i), 0, 0))] + y_specs
                 + [resident(gpost), resident(gpre), resident(gmlp), resident(wo), resident(w1), resident(w2)],
        out_specs=rowblk(D_MODEL),
        compiler_params=_params("arbitrary"),
        name="outproj_mlp",
    )(xs, mods_l, *y_args, gpost, gpre, gmlp, wo, w1, w2)


def _rope_rot_cols(w):
    w4 = w.reshape(w.shape[:-1] + (2, 2, MLA_ROPE // 4))
    return jnp.stack([-w4[..., 1, :], w4[..., 0, :]], axis=-2).reshape(w.shape)


def _zeros_like_cols(w, n):
    return jnp.zeros(w.shape[:-1] + (n,), w.dtype)


def _layout_w_in(w_in):
    o = np.concatenate([[0], np.cumsum(IN_SIZES)])
    qkv, z, beta, dec, lx, lg, cq, ckv, kr = [w_in[..., o[i]:o[i + 1]] for i in range(len(IN_SIZES))]
    zc = functools.partial(_zeros_like_cols, w_in)
    main = jnp.concatenate([qkv, z, lx, lg], axis=-1)
    pad = LANES - MLA_NOPE - MLA_ROPE
    rest = jnp.concatenate([cq, ckv, beta, dec, zc(LANES - 4 * GDN_HEADS),
                            zc(MLA_NOPE), kr, zc(pad), zc(MLA_NOPE), _rope_rot_cols(kr), zc(pad)], axis=-1)
    return main.astype(BF16), rest.astype(BF16)


def _layout_mla(w_uq, w_ukv):
    lead = w_uq.shape[:-1]
    q4 = w_uq.reshape(lead + (MLA_HEADS, MLA_NOPE + MLA_ROPE))
    nope, rope = q4[..., :MLA_NOPE], q4[..., MLA_NOPE:]
    pad = LANES - MLA_NOPE - MLA_ROPE
    wa = jnp.concatenate([nope, rope, _zeros_like_cols(rope, pad)], axis=-1)
    wb = jnp.concatenate([jnp.zeros_like(nope), _rope_rot_cols(rope), _zeros_like_cols(rope, pad)], axis=-1)
    wq = jnp.concatenate([wa.reshape(lead + (-1,)), wb.reshape(lead + (-1,))], axis=-1)
    lead = w_ukv.shape[:-1]
    kv4 = w_ukv.reshape(lead + (MLA_HEADS, MLA_NOPE + MLA_V))
    k_nope, v = kv4[..., :MLA_NOPE], kv4[..., MLA_NOPE:]
    wkn = jnp.concatenate([k_nope, _zeros_like_cols(k_nope, LANES - MLA_NOPE)], axis=-1).reshape(lead + (-1,))
    return wq.astype(BF16), wkn.astype(BF16), v.reshape(lead + (-1,)).astype(BF16)


def _rope_table(seq):
    rows = seq // GRID_W
    row = jnp.repeat(jnp.arange(rows, dtype=F32), GRID_W)
    col = jnp.tile(jnp.arange(GRID_W, dtype=F32), rows)
    half = MLA_ROPE // 2
    inv = ROPE_BASE ** (-jnp.arange(0, half, 2, dtype=F32) / half)
    ang = jnp.stack([row[:, None] * inv, col[:, None] * inv], axis=1)
    ang = jnp.concatenate([ang, ang], axis=-1).reshape(seq, MLA_ROPE)
    cos = jnp.concatenate([jnp.cos(ang), jnp.ones((TM, MLA_ROPE), F32)], axis=0)
    sin = jnp.concatenate([jnp.sin(ang), jnp.zeros((TM, MLA_ROPE), F32)], axis=0)
    n = seq + TM
    one, zero = jnp.ones((n, MLA_NOPE), F32), jnp.zeros((n, MLA_NOPE), F32)
    pad = jnp.zeros((n, LANES - MLA_NOPE - MLA_ROPE), F32)
    scale = (MLA_NOPE + MLA_ROPE) ** -0.5 * math.log2(math.e)
    return jnp.concatenate([scale * one, scale * cos, pad, zero, scale * sin, pad,
                            zero, cos, pad, zero, sin, pad], axis=1)


def _layout_lru(w_a, b_a, w_i, b_i):
    def blockdiag(w):
        g, n = w.shape[1], w.shape[2]
        out = jnp.zeros((w.shape[0], g * n, g * n), w.dtype)
        for j in range(g):
            out = out.at[:, j * n:(j + 1) * n, j * n:(j + 1) * n].set(w[:, j])
        return out

    da, di = blockdiag(w_a), blockdiag(w_i)
    w = jnp.concatenate([da[0], di[0], da[1], di[1]], axis=-1).astype(BF16)
    b = jnp.concatenate([b_a[0], b_i[0], b_a[1], b_i[1]], axis=-1)[None, :]
    return w, b


def kernel(x, c, ctx, c_ctx, w_ada, b_ada, g_attn_pre, g_attn_post, g_mlp_pre, g_mlp_post, w_in, gdn_conv_w,
           gdn_a_log, gdn_dt_bias, gdn_norm_w, lru_conv_w, lru_conv_b, lru_w_a, lru_b_a, lru_w_i, lru_b_i,
           lru_lambda, mla_q_norm, mla_w_uq, mla_kv_norm, mla_w_ukv, w_out, w_mlp1, w_mlp2):
    batch, seq, _ = x.shape
    n_ctx = ctx.shape[1]
    depth = w_ada.shape[0]
    assert seq % TM == 0 and (batch * n_ctx) % TM == 0 and n_ctx % CONV_ROWS == 0 and seq % GRID_W == 0
    assert (batch * seq) % n_ctx == 0 and batch + 1 <= 2 * SUBLANES
    dims = dict(batch=batch, seq=seq, ctx=n_ctx)

    cond = jnp.concatenate([c, c_ctx[None, :], jnp.zeros((2 * SUBLANES - batch - 1, D_MODEL), F32)], axis=0)
    mods = _ada_call(cond, w_ada, b_ada).reshape(depth, 2 * SUBLANES, 6, D_MODEL)
    tab = _rope_table(seq)
    xs = jnp.concatenate([x.reshape(batch * seq, D_MODEL), ctx.reshape(batch * n_ctx, D_MODEL)], axis=0)
    row = lambda a: a[None, :]

    for l in range(depth):
        last = l == depth - 1
        wm, wr = _layout_w_in(w_in[l])
        wq, wkn, wv = _layout_mla(mla_w_uq[l], mla_w_ukv[l])
        pad = jnp.zeros((LANES - 4 * GDN_HEADS,), F32)
        ald = jnp.stack([jnp.concatenate([jnp.zeros((2 * GDN_HEADS,), F32), gdn_a_log[l].reshape(-1), pad]),
                         jnp.concatenate([jnp.zeros((2 * GDN_HEADS,), F32), gdn_dt_bias[l].reshape(-1), pad])])
        p, gg, q, kt, v = _inproj_call(xs, mods[l], row(g_attn_pre[l]), wm, wr, row(mla_q_norm[l]), wq,
                                      row(mla_kv_norm[l]), wkn, wv, tab, ald, batch=batch, seq=seq)
        y_gdn = _gdn_call(p, gg, gdn_conv_w[l], row(gdn_norm_w[l]), **dims)
        w_blk, b_blk = _layout_lru(lru_w_a[l], lru_b_a[l], lru_w_i[l], lru_b_i[l])
        y_lru = _lru_call(p, lru_conv_w[l], row(lru_conv_b[l]), w_blk, b_blk, lru_lambda[l], **dims)
        y_att = (_attn_call(q, kt, v, latent=True, **dims),
                 None if last else _attn_call(q, kt, v, latent=False, **dims))
        xs = _outmlp_call(xs, mods[l], (y_gdn, y_lru, y_att), row(g_attn_post[l]), row(g_mlp_pre[l]),
                          row(g_mlp_post[l]), w_out[l].astype(BF16), w_mlp1[l].astype(BF16),
                          w_mlp2[l].astype(BF16), batch=batch, seq=seq, with_ctx=not last)
    return xs.reshape(batch, seq, D_MODEL)
```

```python
import functools
import math

import numpy as np
import jax
import jax.numpy as jnp
from jax import lax
from jax.experimental import pallas as pl
from jax.experimental.pallas import tpu as pltpu

F32 = jnp.float32
BF16 = jnp.bfloat16

D_MODEL = 1024
EPS = 1e-6
GRID_W = 64
CONV_WIDTH = 4
GDN_HEAD_DIM = 128
GDN_HEADS = 4
GDN_WIDTH = GDN_HEADS * GDN_HEAD_DIM
GDN_CHUNK = 64
LRU_WIDTH = 256
LRU_BLOCKS = 4
LRU_C = 8.0
MLA_V = 64
MLA_HEADS = 4
MLA_WIDTH = MLA_HEADS * MLA_V
MLA_NOPE = 64
MLA_ROPE = 32
MLA_Q_RANK = 256
MLA_KV_RANK = 128
ROPE_BASE = 10000.0
D_FF = 4 * D_MODEL
IN_SIZES = (3 * GDN_WIDTH, GDN_WIDTH, 2 * GDN_HEADS, 2 * GDN_HEADS, LRU_WIDTH, LRU_WIDTH,
            MLA_Q_RANK, MLA_KV_RANK, MLA_ROPE)

LANES = 128
SUBLANES = 8
TM = 512
MAIN_COLS = 3 * GDN_WIDTH + GDN_WIDTH + 2 * LRU_WIDTH
REST_COLS = MLA_Q_RANK + MLA_KV_RANK + 3 * LANES
CONV_ROWS = 256
ATT_TQ = 256
ATT_KB = 512
FF_CHUNK = 1024
VMEM_LIMIT = 56 * 1024 * 1024

_NT = (((1,), (1,)), ((), ()))
_TN = (((0,), (0,)), ((), ()))


def _rms(x, g):
    return x * lax.rsqrt(jnp.mean(x * x, axis=-1, keepdims=True) + EPS) * g


def _bdot(a, b):
    return jnp.dot(a.astype(BF16), b.astype(BF16), preferred_element_type=F32)


def _params(*sem):
    return pltpu.CompilerParams(dimension_semantics=sem, vmem_limit_bytes=VMEM_LIMIT)


def _ada_kernel(c_ref, w_ref, b_ref, o_ref):
    o_ref[...] = _bdot(jax.nn.silu(c_ref[...]), w_ref[...]) + b_ref[...]


def _ada_call(cond, w_ada, b_ada):
    depth = w_ada.shape[0]
    rows = cond.shape[0]
    return pl.pallas_call(
        _ada_kernel,
        out_shape=jax.ShapeDtypeStruct((depth, rows, 6 * D_MODEL), F32),
        grid=(depth, 6),
        in_specs=[pl.BlockSpec((rows, D_MODEL), lambda l, j: (0, 0)),
                  pl.BlockSpec((None, D_MODEL, D_MODEL), lambda l, j: (l, 0, j)),
                  pl.BlockSpec((None, 1, D_MODEL), lambda l, j: (l, 0, j))],
        out_specs=pl.BlockSpec((None, rows, D_MODEL), lambda l, j: (l, 0, j)),
        compiler_params=_params("arbitrary", "arbitrary"),
        name="adaln",
    )(cond, w_ada, b_ada.reshape(depth, 1, 6 * D_MODEL))


def _gdn_gates(raw, a_log, dt_b):
    rows = raw.shape[0]
    lane = lax.broadcasted_iota(jnp.int32, raw.shape, 1)
    pos = lax.broadcasted_iota(jnp.int32, raw.shape, 0) % GDN_CHUNK
    g = -jnp.exp(a_log) * jax.nn.softplus(raw + dt_b)
    cum_f, cum_b = g, g
    s = 1
    while s < GDN_CHUNK:
        cum_f = cum_f + jnp.where(pos >= s, pltpu.roll(cum_f, s, axis=0), 0.0)
        cum_b = cum_b + jnp.where(pos < GDN_CHUNK - s, pltpu.roll(cum_b, rows - s, axis=0), 0.0)
        s *= 2
    return jnp.where(lane < 2 * GDN_HEADS, jax.nn.sigmoid(raw), jnp.where(lane < 3 * GDN_HEADS, cum_f, cum_b))


def _inproj_kernel(x_ref, mod_ref, g_ref, wm_ref, wr_ref, qn_ref, wq_ref, kvn_ref, wkn_ref, wv_ref, tab_ref, ald_ref,
                   p_ref, gg_ref, q_ref, kt_ref, v_ref):
    h = _rms(x_ref[...], g_ref[...]) * (1.0 + mod_ref[1:2, :]) + mod_ref[0:1, :]
    hb = h.astype(BF16)
    p_ref[...] = jnp.dot(hb, wm_ref[...], preferred_element_type=F32)
    r = jnp.dot(hb, wr_ref[...], preferred_element_type=F32)
    o0 = MLA_Q_RANK
    o1 = o0 + MLA_KV_RANK
    cq = r[:, :o0]
    ckv = r[:, o0:o1]
    gg_ref[...] = _gdn_gates(r[:, o1:o1 + LANES], ald_ref[0:1, :], ald_ref[1:2, :])
    kr_a = r[:, o1 + LANES:o1 + 2 * LANES]
    kr_b = r[:, o1 + 2 * LANES:o1 + 3 * LANES]
    tab = tab_ref[...]
    tile = lambda t: jnp.concatenate([t] * MLA_HEADS, axis=1)
    q2 = _bdot(_rms(cq, qn_ref[...]), wq_ref[...])
    hw = MLA_HEADS * LANES
    q_ref[...] = (q2[:, :hw] * tile(tab[:, :LANES]) + q2[:, hw:] * tile(tab[:, LANES:2 * LANES])).astype(BF16)
    ckvn = _rms(ckv, kvn_ref[...]).astype(BF16)
    kr = kr_a * tab[:, 2 * LANES:3 * LANES] + kr_b * tab[:, 3 * LANES:]
    kt_ref[...] = (jnp.dot(ckvn, wkn_ref[...], preferred_element_type=F32) + tile(kr)).T.astype(BF16)
    v_ref[...] = jnp.dot(ckvn, wv_ref[...], preferred_element_type=F32).astype(BF16)


def _inproj_call(xs, mods_l, g, wm, wr, qn, wq, kvn, wkn, wv, tab, ald, *, batch, seq):
    n = xs.shape[0]
    nblk = n // TM
    nlb = batch * seq // TM
    per_b = seq // TM
    modrow = lambda i: jnp.where(i < nlb, i // per_b, batch)
    tabrow = lambda i: jnp.where(i < nlb, i % per_b, per_b)
    full = lambda a: pl.BlockSpec(a.shape, lambda i: (0,) * a.ndim)
    rowblk = lambda w: pl.BlockSpec((TM, w), lambda i: (i, 0))
    hw = MLA_HEADS * LANES
    return pl.pallas_call(
        _inproj_kernel,
        out_shape=(jax.ShapeDtypeStruct((n, MAIN_COLS), F32),
                   jax.ShapeDtypeStruct((n, LANES), F32),
                   jax.ShapeDtypeStruct((n, hw), BF16),
                   jax.ShapeDtypeStruct((hw, n), BF16),
                   jax.ShapeDtypeStruct((n, MLA_WIDTH), BF16)),
        grid=(nblk,),
        in_specs=[rowblk(D_MODEL),
                  pl.BlockSpec((None, 6, D_MODEL), lambda i: (modrow(i), 0, 0)),
                  full(g), full(wm), full(wr), full(qn), full(wq), full(kvn), full(wkn), full(wv),
                  pl.BlockSpec((TM, 4 * LANES), lambda i: (tabrow(i), 0)), full(ald)],
        out_specs=(rowblk(MAIN_COLS), rowblk(LANES), rowblk(hw), pl.BlockSpec((hw, TM), lambda i: (0, i)),
                   rowblk(MLA_WIDTH)),
        compiler_params=_params("arbitrary"),
        name="inproj",
    )(xs, mods_l, g, wm, wr, qn, wq, kvn, wkn, wv, tab, ald)


def _conv_chunk(src_ref, r, n_chunks, n_rows, w, rows=CONV_ROWS):
    start = pl.multiple_of(r * rows, rows)
    cur = src_ref[pl.ds(start, rows), :]
    prev_start = pl.multiple_of(jnp.maximum(start - SUBLANES, 0), SUBLANES)
    prev = jnp.where(r > 0, src_ref[pl.ds(prev_start, SUBLANES), :], 0.0)
    next_start = pl.multiple_of(jnp.minimum(start + rows, n_rows - SUBLANES), SUBLANES)
    nxt = jnp.where(r < n_chunks - 1, src_ref[pl.ds(next_start, SUBLANES), :], 0.0)
    cat = jnp.concatenate([prev, cur, nxt], axis=0)
    tot = rows + 2 * SUBLANES
    sl = slice(SUBLANES, SUBLANES + rows)
    acc = w[2:3, :] * cur
    acc = acc + w[0:1, :] * pltpu.roll(cat, 2, axis=0)[sl]
    acc = acc + w[1:2, :] * pltpu.roll(cat, 1, axis=0)[sl]
    acc = acc + w[3:4, :] * pltpu.roll(cat, tot - 1, axis=0)[sl]
    return start, acc


GDN_UNROLL = 8


def _gdn_kernel(ql_ref, kl_ref, vl_ref, qc_ref, kc_ref, vc_ref, zl_ref, zc_ref, ggl_ref, ggc_ref,
                cwq_ref, cwk_ref, cwv_ref, nw_ref, yl_ref, yc_ref,
                qs, ks, vs, gs, o_loc, o_seq, lhs_r, upd_r, gl_r, *, seq, ctx):
    head = pl.program_id(1)
    C = GDN_CHUNK
    C2 = 2 * C
    hd = GDN_HEAD_DIM
    U = GDN_UNROLL
    n_ctx = ctx // C
    n_tot = (ctx + seq) // C
    pad = U - n_ctx
    n_groups = (n_tot - n_ctx) // U

    cws = (cwq_ref[...], cwk_ref[...], cwv_ref[...])

    def conv_block(srcs, j, n_rows, off):
        for src, dst, w, kind in zip(srcs, (qs, ks, vs), cws, "qkv"):
            start, acc = _conv_chunk(src, j, n_rows // CONV_ROWS, n_rows, w)
            y = jax.nn.silu(acc)
            if kind != "v":
                y = y * lax.rsqrt(jnp.sum(y * y, axis=-1, keepdims=True) + EPS)
            if kind == "q":
                y = y * (hd ** -0.5)
            dst[pl.ds(off + start, CONV_ROWS), :] = y

    def prep_gates(src_ref, off, n_rows):
        def body(r, carry):
            start = pl.multiple_of(r * CONV_ROWS, CONV_ROWS)
            gs[pl.ds(off + start, CONV_ROWS), :] = src_ref[pl.ds(start, CONV_ROWS), :]
            return carry

        lax.fori_loop(0, n_rows // CONV_ROWS, body, 0)

    prep_gates(ggc_ref, 0, ctx)
    prep_gates(ggl_ref, ctx, seq)
    lat_srcs = (ql_ref, kl_ref, vl_ref)
    n_blk = seq // CONV_ROWS
    per_side = U * C // CONV_ROWS
    for j in range(ctx // CONV_ROWS):
        conv_block((qc_ref, kc_ref, vc_ref), j, ctx, 0)
    for j in sorted(set(range(min(per_side, n_blk))) | set(range(max(n_blk - per_side, 0), n_blk))):
        conv_block(lat_srcs, j, seq, ctx)
    conv_groups = max(n_blk // (2 * per_side) - 1, 0)

    lane = lax.broadcasted_iota(jnp.int32, (C, LANES), 1)
    sub_t = lax.broadcasted_iota(jnp.int32, (LANES, C), 0)
    bwd_lane = (lane % (2 * GDN_HEADS)) >= GDN_HEADS
    prow = lax.broadcasted_iota(jnp.int32, (C2, C2), 0)
    pcol = lax.broadcasted_iota(jnp.int32, (C2, C2), 1)
    sgn = 1 - 2 * (prow // C)
    incl_p = ((prow // C) == (pcol // C)) & ((prow - pcol) * sgn >= 0)
    eye_p = (prow == pcol).astype(F32)
    off_diag = []
    g = 1
    while g < C:
        half = (prow % (2 * g)) // g - (pcol % (2 * g)) // g
        off_diag.append(((prow // (2 * g)) == (pcol // (2 * g))) & (half * sgn == 1))
        g *= 2

    def step_chunks(s):
        return s, jnp.where(s < n_ctx, n_ctx - 1 - s, n_tot - 1 - (s - n_ctx))

    def chunk_rows(c):
        return pl.ds(pl.multiple_of(c * C, C), C)

    def seq_step(entry, v, carry):
        cf, cb = step_chunks(v - pad)
        chunks = (jnp.where(v < pad, n_tot, cf), jnp.where(v < pad, n_tot, cb))
        new = []
        for d in range(2):
            res = jnp.dot(lhs_r[entry, d], carry[d].astype(BF16), preferred_element_type=F32)
            o_seq[d, chunk_rows(chunks[d]), :] = res[:C]
            new.append(carry[d] * gl_r[entry, d, 0:1, :] + upd_r[entry, d] - res[C:])
        return tuple(new)

    def local_steps(steps, entries, tick, after_loads=lambda: None):
        U_ = range(len(steps))
        q2, k2, v2, g_col, b_col, g_last, decay, rows = [], [], [], [], [], [], [], []
        for s in steps:
            cf, cb = step_chunks(s)
            rf, rb = chunk_rows(cf), chunk_rows(cb)
            rows.append((rf, rb))
            q2.append(jnp.concatenate([qs[rf, :], qs[rb, :]], axis=0))
            k2.append(jnp.concatenate([ks[rf, :], ks[rb, :]], axis=0))
            v2.append(jnp.concatenate([vs[rf, :], vs[rb, :]], axis=0))
            gg = jnp.where(bwd_lane, gs[rb, :], gs[rf, :])
            gg_t = gg.T
            g_cols, b_cols, g_rows = [], [], []
            for d in range(2):
                g_lane = 2 * GDN_HEADS + d * GDN_HEADS + head
                g_cols.append(jnp.sum(jnp.where(lane == g_lane, gg, 0.0), axis=-1, keepdims=True))
                b_cols.append(jnp.sum(jnp.where(lane == d * GDN_HEADS + head, gg, 0.0), axis=-1, keepdims=True))
                g_rows.append(jnp.sum(jnp.where(sub_t == g_lane, gg_t, 0.0), axis=0, keepdims=True))
            g_last.append((g_cols[0][C - 1:C, :], g_cols[1][0:1, :]))
            g_col.append(jnp.concatenate(g_cols, axis=0))
            b_col.append(jnp.concatenate(b_cols, axis=0))
            g_row = jnp.concatenate(g_rows, axis=1)
            decay.append(jnp.exp(jnp.where(incl_p, g_col[-1] - g_row, -jnp.inf)))
        after_loads()
        tick()
        r_qk = [lax.dot_general(jnp.concatenate([q2[u], k2[u]], axis=0).astype(BF16), k2[u].astype(BF16), _NT,
                                preferred_element_type=F32) for u in U_]
        attn = [r_qk[u][:C2] * decay[u] for u in U_]
        low = [r_qk[u][C2:] * b_col[u] * decay[u] for u in U_]
        tick()
        t_inv = [eye_p - jnp.where(off_diag[0], low[u], 0.0) for u in U_]
        for m in off_diag[1:]:
            xt = [_bdot(jnp.where(m, low[u], 0.0), t_inv[u]) for u in U_]
            t_inv = [t_inv[u] - _bdot(t_inv[u], xt[u]) for u in U_]
            tick()
        e_g = [jnp.exp(g_col[u]) for u in U_]
        x = [_bdot(t_inv[u], jnp.concatenate([k2[u] * (e_g[u] * b_col[u]), v2[u] * b_col[u]], axis=1))
             for u in U_]
        tick()
        ax = [_bdot(attn[u], x[u]) for u in U_]
        tick()
        zero = jnp.zeros((C, hd), F32)
        kx = []
        for u in U_:
            gl_col = jnp.concatenate([jnp.broadcast_to(g_last[u][0], (C, 1)),
                                      jnp.broadcast_to(g_last[u][1], (C, 1))], axis=0)
            kg = k2[u] * jnp.exp(gl_col - g_col[u])
            kg_blk = jnp.concatenate([jnp.concatenate([kg[:C], zero], axis=1),
                                      jnp.concatenate([zero, kg[C:]], axis=1)], axis=0)
            kx.append(lax.dot_general(kg_blk.astype(BF16), x[u].astype(BF16), _TN,
                                      preferred_element_type=F32))
        tick()
        for u in U_:
            q_eff = q2[u] * e_g[u] - ax[u][:, :hd]
            for d in range(2):
                lhs_r[entries[u], d, 0:C, :] = q_eff[d * C:(d + 1) * C].astype(BF16)
                lhs_r[entries[u], d, C:C + hd, :] = kx[u][d * hd:(d + 1) * hd, :hd].astype(BF16)
                upd_r[entries[u], d] = kx[u][d * hd:(d + 1) * hd, hd:]
                gl_r[entries[u], d] = jnp.broadcast_to(jnp.exp(g_last[u][d]), (SUBLANES, LANES))
                o_loc[d, rows[u][d], :] = ax[u][d * C:(d + 1) * C, hd:]

    for e in range(pad):
        for d in range(2):
            lhs_r[e, d] = jnp.zeros((C + hd, LANES), BF16)
            upd_r[e, d] = jnp.zeros((hd, LANES), F32)
            gl_r[e, d] = jnp.ones((SUBLANES, LANES), F32)
    local_steps(list(range(n_ctx)), [pad + u for u in range(n_ctx)], lambda: None)

    def group(i, carry, with_conv):
        state = {"carry": carry, "next": 0}

        def conv_next():
            for t in range(per_side):
                conv_block(lat_srcs, i * per_side + t, seq, ctx)
                conv_block(lat_srcs, n_blk - 1 - i * per_side - t, seq, ctx)

        prev_base = ((i - 1) % 2) * U

        def tick():
            if state["next"] < U:
                u = state["next"]
                state["carry"] = seq_step(prev_base + u, (i - 1) * U + u, state["carry"])
                state["next"] = u + 1

        base = (i % 2) * U
        local_steps([i * U + u - pad for u in range(U)], [base + u for u in range(U)], tick,
                    conv_next if with_conv else (lambda: None))
        while state["next"] < U:
            tick()
        return state["carry"]

    zero = jnp.zeros((hd, hd), F32)
    carry = lax.fori_loop(1, 1 + conv_groups, functools.partial(group, with_conv=True), (zero, zero))
    carry = lax.fori_loop(1 + conv_groups, n_groups + 1, functools.partial(group, with_conv=False), carry)
    for u in range(U):
        carry = seq_step((n_groups % 2) * U + u, n_groups * U + u, carry)

    def finish(z_ref, y_ref, off, n_rows):
        def body(r, carry):
            start = pl.multiple_of(r * CONV_ROWS, CONV_ROWS)
            rows = pl.ds(off + start, CONV_ROWS)
            o = (o_loc[0, rows, :] + o_loc[1, rows, :]) + (o_seq[0, rows, :] + o_seq[1, rows, :])
            y = _rms(o, nw_ref[...]) * jax.nn.silu(z_ref[pl.ds(start, CONV_ROWS), :])
            y_ref[pl.ds(start, CONV_ROWS), :] = y.astype(y_ref.dtype)
            return carry

        lax.fori_loop(0, n_rows // CONV_ROWS, body, 0)

    finish(zc_ref, yc_ref, 0, ctx)
    finish(zl_ref, yl_ref, ctx, seq)


def _gdn_call(p, gg, conv_w, norm_w, *, batch, seq, ctx):
    H = GDN_HEADS
    C = GDN_CHUNK
    cb0 = batch * seq // ctx
    lat = lambda cblk: pl.BlockSpec((seq, LANES), lambda b, h: (b, cblk * H + h))
    cx = lambda cblk: pl.BlockSpec((ctx, LANES), lambda b, h: (cb0 + b, cblk * H + h))
    cw = lambda cblk: pl.BlockSpec((CONV_WIDTH, LANES), lambda b, h: (0, cblk * H + h))
    T = seq + ctx
    assert ctx // C <= GDN_UNROLL and (seq // C) % GDN_UNROLL == 0
    return pl.pallas_call(
        functools.partial(_gdn_kernel, seq=seq, ctx=ctx),
        out_shape=(jax.ShapeDtypeStruct((batch * seq, GDN_WIDTH), BF16),
                   jax.ShapeDtypeStruct((batch * ctx, GDN_WIDTH), BF16)),
        grid=(batch, H),
        in_specs=[lat(0), lat(1), lat(2), cx(0), cx(1), cx(2), lat(3), cx(3),
                  pl.BlockSpec((seq, LANES), lambda b, h: (b, 0)),
                  pl.BlockSpec((ctx, LANES), lambda b, h: (cb0 + b, 0)),
                  cw(0), cw(1), cw(2),
                  pl.BlockSpec((1, LANES), lambda b, h: (0, 0))],
        out_specs=(pl.BlockSpec((seq, LANES), lambda b, h: (b, h)),
                   pl.BlockSpec((ctx, LANES), lambda b, h: (b, h))),
        scratch_shapes=[pltpu.VMEM((T, LANES), F32)] * 4
                       + [pltpu.VMEM((2, T, LANES), F32), pltpu.VMEM((2, T + C, LANES), F32),
                          pltpu.VMEM((2 * GDN_UNROLL, 2, C + GDN_HEAD_DIM, LANES), BF16),
                          pltpu.VMEM((2 * GDN_UNROLL, 2, GDN_HEAD_DIM, LANES), F32),
                          pltpu.VMEM((2 * GDN_UNROLL, 2, SUBLANES, LANES), F32)],
        compiler_params=_params("arbitrary", "arbitrary"),
        name="gdn",
    )(p, p, p, p, p, p, p, p, gg, gg, conv_w, conv_w, conv_w, norm_w)


def _lru_scan(a, u, fwd):
    n = a.shape[0]
    row = lax.broadcasted_iota(jnp.int32, a.shape, 0)
    s = 1
    while s < n:
        shift = s if fwd else n - s
        valid = (row >= s) if fwd else (row < n - s)
        u = jnp.where(valid, a * pltpu.roll(u, shift, axis=0) + u, u)
        a = jnp.where(valid, a * pltpu.roll(a, shift, axis=0), a)
        s *= 2
    return a, u


def _lru_kernel(xl_ref, xc_ref, gl_ref, gc_ref, cw_ref, cb_ref, w_ref, b_ref, lam_ref, yl_ref, yc_ref,
                ab_s, ub_s, hf_s, *, seq, ctx):
    W = LRU_WIDTH
    neg_log_base = jax.nn.softplus(-lam_ref[...])
    cw = cw_ref[...]

    def forward(src_ref, off, n_rows, h):
        n_chunks = n_rows // CONV_ROWS

        def body(r, h):
            start, acc = _conv_chunk(src_ref, r, n_chunks, n_rows, cw)
            xc = acc + cb_ref[...]
            gates = jax.nn.sigmoid(_bdot(xc, w_ref[...]) + b_ref[...])
            coef = []
            for d in range(2):
                rg = gates[:, 2 * d * W:(2 * d + 1) * W]
                ig = gates[:, (2 * d + 1) * W:(2 * d + 2) * W]
                log_a = -LRU_C * rg * neg_log_base[d:d + 1, :]
                a = jnp.exp(log_a)
                coef.append((a, jnp.sqrt(-jnp.tanh(log_a) * (a * a + 1.0)) * (ig * xc)))
            rows = pl.ds(off + start, CONV_ROWS)
            ab_s[rows, :] = coef[1][0]
            ub_s[rows, :] = coef[1][1]
            a_cum, h0 = _lru_scan(coef[0][0], coef[0][1], True)
            hf = h0 + a_cum * h
            hf_s[rows, :] = hf
            return hf[CONV_ROWS - 1:CONV_ROWS, :]

        return lax.fori_loop(0, n_chunks, body, h)

    def backward(gate_ref, y_ref, off, n_rows, h):
        n_chunks = n_rows // CONV_ROWS

        def body(i, h):
            r = n_chunks - 1 - i
            start = pl.multiple_of(r * CONV_ROWS, CONV_ROWS)
            rows = pl.ds(off + start, CONV_ROWS)
            a_cum, h0 = _lru_scan(ab_s[rows, :], ub_s[rows, :], False)
            hb = h0 + a_cum * h
            y = (hf_s[rows, :] + hb) * jax.nn.gelu(gate_ref[pl.ds(start, CONV_ROWS), :])
            y_ref[pl.ds(start, CONV_ROWS), :] = y.astype(y_ref.dtype)
            return hb[0:1, :]

        return lax.fori_loop(0, n_chunks, body, h)

    zero = jnp.zeros((1, W), F32)
    forward(xl_ref, ctx, seq, forward(xc_ref, 0, ctx, zero))
    backward(gl_ref, yl_ref, ctx, seq, backward(gc_ref, yc_ref, 0, ctx, zero))


def _lru_call(p, conv_w, conv_b, w_blk, b_blk, lam, *, batch, seq, ctx):
    cb0 = batch * seq // ctx
    x_col = (3 * GDN_WIDTH + GDN_WIDTH) // LRU_WIDTH
    T = seq + ctx
    full = lambda a: pl.BlockSpec(a.shape, lambda b: (0,) * a.ndim)
    return pl.pallas_call(
        functools.partial(_lru_kernel, seq=seq, ctx=ctx),
        out_shape=(jax.ShapeDtypeStruct((batch * seq, LRU_WIDTH), BF16),
                   jax.ShapeDtypeStruct((batch * ctx, LRU_WIDTH), BF16)),
        grid=(batch,),
        in_specs=[pl.BlockSpec((seq, LRU_WIDTH), lambda b: (b, x_col)),
                  pl.BlockSpec((ctx, LRU_WIDTH), lambda b: (cb0 + b, x_col)),
                  pl.BlockSpec((seq, LRU_WIDTH), lambda b: (b, x_col + 1)),
                  pl.BlockSpec((ctx, LRU_WIDTH), lambda b: (cb0 + b, x_col + 1)),
                  full(conv_w), full(conv_b), full(w_blk), full(b_blk), full(lam)],
        out_specs=(pl.BlockSpec((seq, LRU_WIDTH), lambda b: (b, 0)),
                   pl.BlockSpec((ctx, LRU_WIDTH), lambda b: (b, 0))),
        scratch_shapes=[pltpu.VMEM((T, LRU_WIDTH), F32)] * 3,
        compiler_params=_params("arbitrary"),
        name="rglru",
    )(p, p, p, p, conv_w, conv_b, w_blk, b_blk, lam)


def _attn_kernel(*refs, n_src):
    q_ref = refs[0]
    srcs = [(refs[1 + 2 * i], refs[2 + 2 * i]) for i in range(n_src)]
    o_ref, s_scr, p_scr = refs[1 + 2 * n_src:]
    tq = q_ref.shape[0]
    lane = lax.broadcasted_iota(jnp.int32, (tq, LANES), 1)
    per_vreg = LANES // MLA_V
    col_blocks = []
    off = 0
    for i, (kt_ref, _) in enumerate(srcs):
        n = kt_ref.shape[1]
        for st in range(0, n, ATT_KB):
            col_blocks.append((i, st, off + st, min(ATT_KB, n - st)))
        off += n

    def fold(acc, x, op):
        for i in range(x.shape[1] // LANES):
            acc = op(acc, x[:, i * LANES:(i + 1) * LANES])
        return acc

    def scores(h):
        hrows = slice(h * LANES, (h + 1) * LANES)
        qh = q_ref[:, hrows]
        m_part = jnp.full((tq, LANES), -jnp.inf, F32)
        for i, st, dst, w in col_blocks:
            s = jnp.dot(qh, srcs[i][0][hrows, st:st + w], preferred_element_type=F32)
            s_scr[h % 2, :, dst:dst + w] = s
            m_part = fold(m_part, s, jnp.maximum)
        return jnp.max(m_part, axis=-1, keepdims=True)

    def probs(h, m):
        l_part = jnp.zeros((tq, LANES), F32)
        for _, _, dst, w in col_blocks:
            p = jnp.exp2(s_scr[h % 2, :, dst:dst + w] - m)
            p_scr[h % 2, :, dst:dst + w] = p.astype(BF16)
            l_part = fold(l_part, p, jnp.add)
        return jnp.sum(l_part, axis=-1, keepdims=True)

    def weighted(h, l_sum):
        vcols = slice((h // per_vreg) * LANES, (h // per_vreg + 1) * LANES)
        acc = None
        off = 0
        for _, v_ref in srcs:
            n = v_ref.shape[0]
            part = jnp.dot(p_scr[h % 2, :, off:off + n], v_ref[:, vcols], preferred_element_type=F32)
            acc = part if acc is None else acc + part
            off += n
        return acc * (1.0 / l_sum)

    outs = []
    m_next = scores(0)
    for h in range(MLA_HEADS):
        m = m_next
        if h + 1 < MLA_HEADS:
            m_next = scores(h + 1)
        outs.append(weighted(h, probs(h, m)))
    tiles = []
    for t in range(MLA_HEADS // per_vreg):
        tile = outs[t * per_vreg]
        for j in range(1, per_vreg):
            tile = jnp.where(lane >= j * MLA_V, outs[t * per_vreg + j], tile)
        tiles.append(tile)
    o_ref[...] = jnp.concatenate(tiles, axis=1).astype(o_ref.dtype)


def _attn_call(q, kt, v, *, batch, seq, ctx, latent):
    hw = MLA_HEADS * LANES
    cb0 = batch * seq // ctx
    kv_specs = [pl.BlockSpec((hw, ctx), lambda b, i: (0, cb0 + b)),
                pl.BlockSpec((ctx, MLA_WIDTH), lambda b, i: (cb0 + b, 0))]
    args = (q, kt, v)
    if latent:
        per_b = seq // ATT_TQ
        grid = (batch, per_b)
        q_spec = pl.BlockSpec((ATT_TQ, hw), lambda b, i: (b * per_b + i, 0))
        kv_specs += [pl.BlockSpec((hw, seq), lambda b, i: (0, b)),
                     pl.BlockSpec((seq, MLA_WIDTH), lambda b, i: (b, 0))]
        args += (kt, v)
        out_spec = pl.BlockSpec((ATT_TQ, MLA_WIDTH), lambda b, i: (b * per_b + i, 0))
        rows, tq, keys = batch * seq, ATT_TQ, ctx + seq
    else:
        grid = (batch, 1)
        q_spec = pl.BlockSpec((ctx, hw), lambda b, i: (cb0 + b, 0))
        out_spec = pl.BlockSpec((ctx, MLA_WIDTH), lambda b, i: (b, 0))
        rows, tq, keys = batch * ctx, ctx, ctx
    return pl.pallas_call(
        functools.partial(_attn_kernel, n_src=len(kv_specs) // 2),
        out_shape=jax.ShapeDtypeStruct((rows, MLA_WIDTH), BF16),
        grid=grid,
        in_specs=[q_spec] + kv_specs,
        out_specs=out_spec,
        scratch_shapes=[pltpu.VMEM((2, tq, keys), F32), pltpu.VMEM((2, tq, keys), BF16)],
        compiler_params=_params("arbitrary", "arbitrary"),
        name="mla_lat" if latent else "mla_ctx",
    )(*args)


def _outmlp_kernel(*refs, with_ctx, n_lat_blocks):
    x_ref, mod_ref = refs[0], refs[1]
    n_y = 6 if with_ctx else 3
    y_refs = refs[2:2 + n_y]
    gpost_ref, gpre_ref, gmlp_ref, wo_ref, w1_ref, w2_ref, o_ref = refs[2 + n_y:]
    if with_ctx:
        is_lat = pl.program_id(0) < n_lat_blocks
        parts = [jnp.where(is_lat, y_refs[2 * i][...], y_refs[2 * i + 1][...]) for i in range(3)]
    else:
        parts = [r[...] for r in y_refs]
    y = jnp.concatenate(parts, axis=1)
    t = jnp.dot(y, wo_ref[...], preferred_element_type=F32)
    x1 = x_ref[...] + mod_ref[2:3, :] * _rms(t, gpost_ref[...])
    h2 = (_rms(x1, gpre_ref[...]) * (1.0 + mod_ref[4:5, :]) + mod_ref[3:4, :]).astype(BF16)
    acc = jnp.zeros(x1.shape, F32)
    for f in range(D_FF // FF_CHUNK):
        cols = slice(f * FF_CHUNK, (f + 1) * FF_CHUNK)
        a = jnp.dot(h2, w1_ref[:, cols], preferred_element_type=F32)
        a = jnp.square(jnp.maximum(a, 0.0)).astype(BF16)
        acc = acc + jnp.dot(a, w2_ref[cols, :], preferred_element_type=F32)
    o_ref[...] = x1 + mod_ref[5:6, :] * _rms(acc, gmlp_ref[...])


def _outmlp_call(xs, mods_l, ys, gpost, gpre, gmlp, wo, w1, w2, *, batch, seq, with_ctx):
    n = xs.shape[0]
    nlb = batch * seq // TM
    nblk = n // TM if with_ctx else nlb
    per_b = seq // TM
    modrow = lambda i: jnp.where(i < nlb, i // per_b, batch)
    rowblk = lambda w: pl.BlockSpec((TM, w), lambda i: (i, 0))
    resident = lambda a: pl.BlockSpec(a.shape, lambda i: (0,) * a.ndim, pipeline_mode=pl.Buffered(1))
    y_specs, y_args = [], []
    for y_lat, y_ctx in ys:
        w = y_lat.shape[1]
        y_specs.append(pl.BlockSpec((TM, w), lambda i: (jnp.minimum(i, nlb - 1), 0)))
        y_args.append(y_lat)
        if with_ctx:
            y_specs.append(pl.BlockSpec((TM, w), lambda i: (jnp.maximum(i - nlb, 0), 0)))
            y_args.append(y_ctx)
    return pl.pallas_call(
        functools.partial(_outmlp_kernel, with_ctx=with_ctx, n_lat_blocks=nlb),
        out_shape=jax.ShapeDtypeStruct((nblk * TM, D_MODEL), F32),
        grid=(nblk,),
        in_specs=[rowblk(D_MODEL), pl.BlockSpec((None, 6, D_MODEL), lambda i: (modrow(i), 0, 0))] + y_specs
                 + [resident(gpost), resident(gpre), resident(gmlp), resident(wo), resident(w1), resident(w2)],
        out_specs=rowblk(D_MODEL),
        compiler_params=_params("arbitrary"),
        name="outproj_mlp",
    )(xs, mods_l, *y_args, gpost, gpre, gmlp, wo, w1, w2)


def _rope_rot_cols(w):
    w4 = w.reshape(w.shape[:-1] + (2, 2, MLA_ROPE // 4))
    return jnp.stack([-w4[..., 1, :], w4[..., 0, :]], axis=-2).reshape(w.shape)


def _zeros_like_cols(w, n):
    return jnp.zeros(w.shape[:-1] + (n,), w.dtype)


def _layout_w_in(w_in):
    o = np.concatenate([[0], np.cumsum(IN_SIZES)])
    qkv, z, beta, dec, lx, lg, cq, ckv, kr = [w_in[..., o[i]:o[i + 1]] for i in range(len(IN_SIZES))]
    zc = functools.partial(_zeros_like_cols, w_in)
    main = jnp.concatenate([qkv, z, lx, lg], axis=-1)
    pad = LANES - MLA_NOPE - MLA_ROPE
    rest = jnp.concatenate([cq, ckv, beta, dec, zc(LANES - 4 * GDN_HEADS),
                            zc(MLA_NOPE), kr, zc(pad), zc(MLA_NOPE), _rope_rot_cols(kr), zc(pad)], axis=-1)
    return main.astype(BF16), rest.astype(BF16)


def _layout_mla(w_uq, w_ukv):
    lead = w_uq.shape[:-1]
    q4 = w_uq.reshape(lead + (MLA_HEADS, MLA_NOPE + MLA_ROPE))
    nope, rope = q4[..., :MLA_NOPE], q4[..., MLA_NOPE:]
    pad = LANES - MLA_NOPE - MLA_ROPE
    wa = jnp.concatenate([nope, rope, _zeros_like_cols(rope, pad)], axis=-1)
    wb = jnp.concatenate([jnp.zeros_like(nope), _rope_rot_cols(rope), _zeros_like_cols(rope, pad)], axis=-1)
    wq = jnp.concatenate([wa.reshape(lead + (-1,)), wb.reshape(lead + (-1,))], axis=-1)
    lead = w_ukv.shape[:-1]
    kv4 = w_ukv.reshape(lead + (MLA_HEADS, MLA_NOPE + MLA_V))
    k_nope, v = kv4[..., :MLA_NOPE], kv4[..., MLA_NOPE:]
    wkn = jnp.concatenate([k_nope, _zeros_like_cols(k_nope, LANES - MLA_NOPE)], axis=-1).reshape(lead + (-1,))
    return wq.astype(BF16), wkn.astype(BF16), v.reshape(lead + (-1,)).astype(BF16)


def _rope_table(seq):
    rows = seq // GRID_W
    row = jnp.repeat(jnp.arange(rows, dtype=F32), GRID_W)
    col = jnp.tile(jnp.arange(GRID_W, dtype=F32), rows)
    half = MLA_ROPE // 2
    inv = ROPE_BASE ** (-jnp.arange(0, half, 2, dtype=F32) / half)
    ang = jnp.stack([row[:, None] * inv, col[:, None] * inv], axis=1)
    ang = jnp.concatenate([ang, ang], axis=-1).reshape(seq, MLA_ROPE)
    cos = jnp.concatenate([jnp.cos(ang), jnp.ones((TM, MLA_ROPE), F32)], axis=0)
    sin = jnp.concatenate([jnp.sin(ang), jnp.zeros((TM, MLA_ROPE), F32)], axis=0)
    n = seq + TM
    one, zero = jnp.ones((n, MLA_NOPE), F32), jnp.zeros((n, MLA_NOPE), F32)
    pad = jnp.zeros((n, LANES - MLA_NOPE - MLA_ROPE), F32)
    scale = (MLA_NOPE + MLA_ROPE) ** -0.5 * math.log2(math.e)
    return jnp.concatenate([scale * one, scale * cos, pad, zero, scale * sin, pad,
                            zero, cos, pad, zero, sin, pad], axis=1)


def _layout_lru(w_a, b_a, w_i, b_i):
    def blockdiag(w):
        g, n = w.shape[1], w.shape[2]
        out = jnp.zeros((w.shape[0], g * n, g * n), w.dtype)
        for j in range(g):
            out = out.at[:, j * n:(j + 1) * n, j * n:(j + 1) * n].set(w[:, j])
        return out

    da, di = blockdiag(w_a), blockdiag(w_i)
    w = jnp.concatenate([da[0], di[0], da[1], di[1]], axis=-1).astype(BF16)
    b = jnp.concatenate([b_a[0], b_i[0], b_a[1], b_i[1]], axis=-1)[None, :]
    return w, b


def kernel(x, c, ctx, c_ctx, w_ada, b_ada, g_attn_pre, g_attn_post, g_mlp_pre, g_mlp_post, w_in, gdn_conv_w,
           gdn_a_log, gdn_dt_bias, gdn_norm_w, lru_conv_w, lru_conv_b, lru_w_a, lru_b_a, lru_w_i, lru_b_i,
           lru_lambda, mla_q_norm, mla_w_uq, mla_kv_norm, mla_w_ukv, w_out, w_mlp1, w_mlp2):
    batch, seq, _ = x.shape
    n_ctx = ctx.shape[1]
    depth = w_ada.shape[0]
    assert seq % TM == 0 and (batch * n_ctx) % TM == 0 and n_ctx % CONV_ROWS == 0 and seq % GRID_W == 0
    assert (batch * seq) % n_ctx == 0 and batch + 1 <= 2 * SUBLANES
    dims = dict(batch=batch, seq=seq, ctx=n_ctx)

    cond = jnp.concatenate([c, c_ctx[None, :], jnp.zeros((2 * SUBLANES - batch - 1, D_MODEL), F32)], axis=0)
    mods = _ada_call(cond, w_ada, b_ada).reshape(depth, 2 * SUBLANES, 6, D_MODEL)
    tab = _rope_table(seq)
    xs = jnp.concatenate([x.reshape(batch * seq, D_MODEL), ctx.reshape(batch * n_ctx, D_MODEL)], axis=0)
    row = lambda a: a[None, :]

    for l in range(depth):
        last = l == depth - 1
        wm, wr = _layout_w_in(w_in[l])
        wq, wkn, wv = _layout_mla(mla_w_uq[l], mla_w_ukv[l])
        pad = jnp.zeros((LANES - 4 * GDN_HEADS,), F32)
        ald = jnp.stack([jnp.concatenate([jnp.zeros((2 * GDN_HEADS,), F32), gdn_a_log[l].reshape(-1), pad]),
                         jnp.concatenate([jnp.zeros((2 * GDN_HEADS,), F32), gdn_dt_bias[l].reshape(-1), pad])])
        p, gg, q, kt, v = _inproj_call(xs, mods[l], row(g_attn_pre[l]), wm, wr, row(mla_q_norm[l]), wq,
                                      row(mla_kv_norm[l]), wkn, wv, tab, ald, batch=batch, seq=seq)
        y_gdn = _gdn_call(p, gg, gdn_conv_w[l], row(gdn_norm_w[l]), **dims)
        w_blk, b_blk = _layout_lru(lru_w_a[l], lru_b_a[l], lru_w_i[l], lru_b_i[l])
        y_lru = _lru_call(p, lru_conv_w[l], row(lru_conv_b[l]), w_blk, b_blk, lru_lambda[l], **dims)
        y_att = (_attn_call(q, kt, v, latent=True, **dims),
                 None if last else _attn_call(q, kt, v, latent=False, **dims))
        xs = _outmlp_call(xs, mods[l], (y_gdn, y_lru, y_att), row(g_attn_post[l]), row(g_mlp_pre[l]),
                          row(g_mlp_post[l]), w_out[l].astype(BF16), w_mlp1[l].astype(BF16),
                          w_mlp2[l].astype(BF16), batch=batch, seq=seq, with_ctx=not last)
    return xs.reshape(batch, seq, D_MODEL)
```

```python
import functools
import math

import numpy as np
import jax
import jax.numpy as jnp
from jax import lax
from jax.experimental import pallas as pl
from jax.experimental.pallas import tpu as pltpu

F32 = jnp.float32
BF16 = jnp.bfloat16

D_MODEL = 1024
EPS = 1e-6
GRID_W = 64
CONV_WIDTH = 4
GDN_HEAD_DIM = 128
GDN_HEADS = 4
GDN_WIDTH = GDN_HEADS * GDN_HEAD_DIM
GDN_CHUNK = 64
LRU_WIDTH = 256
LRU_BLOCKS = 4
LRU_C = 8.0
MLA_V = 64
MLA_HEADS = 4
MLA_WIDTH = MLA_HEADS * MLA_V
MLA_NOPE = 64
MLA_ROPE = 32
MLA_Q_RANK = 256
MLA_KV_RANK = 128
ROPE_BASE = 10000.0
D_FF = 4 * D_MODEL
IN_SIZES = (3 * GDN_WIDTH, GDN_WIDTH, 2 * GDN_HEADS, 2 * GDN_HEADS, LRU_WIDTH, LRU_WIDTH,
            MLA_Q_RANK, MLA_KV_RANK, MLA_ROPE)

LANES = 128
SUBLANES = 8
TM = 512
MAIN_COLS = 3 * GDN_WIDTH + GDN_WIDTH + 2 * LRU_WIDTH
REST_COLS = MLA_Q_RANK + MLA_KV_RANK + 3 * LANES
CONV_ROWS = 256
ATT_TQ = 256
ATT_KB = 512
FF_CHUNK = 1024
VMEM_LIMIT = 56 * 1024 * 1024

_NT = (((1,), (1,)), ((), ()))
_TN = (((0,), (0,)), ((), ()))


def _rms(x, g):
    return x * lax.rsqrt(jnp.mean(x * x, axis=-1, keepdims=True) + EPS) * g


def _bdot(a, b):
    return jnp.dot(a.astype(BF16), b.astype(BF16), preferred_element_type=F32)


def _params(*sem):
    return pltpu.CompilerParams(dimension_semantics=sem, vmem_limit_bytes=VMEM_LIMIT)


def _ada_kernel(c_ref, w_ref, b_ref, o_ref):
    o_ref[...] = _bdot(jax.nn.silu(c_ref[...]), w_ref[...]) + b_ref[...]


def _ada_call(cond, w_ada, b_ada):
    depth = w_ada.shape[0]
    rows = cond.shape[0]
    return pl.pallas_call(
        _ada_kernel,
        out_shape=jax.ShapeDtypeStruct((depth, rows, 6 * D_MODEL), F32),
        grid=(depth, 6),
        in_specs=[pl.BlockSpec((rows, D_MODEL), lambda l, j: (0, 0)),
                  pl.BlockSpec((None, D_MODEL, D_MODEL), lambda l, j: (l, 0, j)),
                  pl.BlockSpec((None, 1, D_MODEL), lambda l, j: (l, 0, j))],
        out_specs=pl.BlockSpec((None, rows, D_MODEL), lambda l, j: (l, 0, j)),
        compiler_params=_params("arbitrary", "arbitrary"),
        name="adaln",
    )(cond, w_ada, b_ada.reshape(depth, 1, 6 * D_MODEL))


def _gdn_gates(raw, a_log, dt_b):
    rows = raw.shape[0]
    lane = lax.broadcasted_iota(jnp.int32, raw.shape, 1)
    pos = lax.broadcasted_iota(jnp.int32, raw.shape, 0) % GDN_CHUNK
    g = -jnp.exp(a_log) * jax.nn.softplus(raw + dt_b)
    cum_f, cum_b = g, g
    s = 1
    while s < GDN_CHUNK:
        cum_f = cum_f + jnp.where(pos >= s, pltpu.roll(cum_f, s, axis=0), 0.0)
        cum_b = cum_b + jnp.where(pos < GDN_CHUNK - s, pltpu.roll(cum_b, rows - s, axis=0), 0.0)
        s *= 2
    return jnp.where(lane < 2 * GDN_HEADS, jax.nn.sigmoid(raw), jnp.where(lane < 3 * GDN_HEADS, cum_f, cum_b))


def _inproj_kernel(x_ref, mod_ref, g_ref, wm_ref, wr_ref, qn_ref, wq_ref, kvn_ref, wkn_ref, wv_ref, tab_ref, ald_ref,
                   p_ref, gg_ref, q_ref, kt_ref, v_ref):
    h = _rms(x_ref[...], g_ref[...]) * (1.0 + mod_ref[1:2, :]) + mod_ref[0:1, :]
    hb = h.astype(BF16)
    p_ref[...] = jnp.dot(hb, wm_ref[...], preferred_element_type=F32)
    r = jnp.dot(hb, wr_ref[...], preferred_element_type=F32)
    o0 = MLA_Q_RANK
    o1 = o0 + MLA_KV_RANK
    cq = r[:, :o0]
    ckv = r[:, o0:o1]
    gg_ref[...] = _gdn_gates(r[:, o1:o1 + LANES], ald_ref[0:1, :], ald_ref[1:2, :])
    kr_a = r[:, o1 + LANES:o1 + 2 * LANES]
    kr_b = r[:, o1 + 2 * LANES:o1 + 3 * LANES]
    tab = tab_ref[...]
    tile = lambda t: jnp.concatenate([t] * MLA_HEADS, axis=1)
    q2 = _bdot(_rms(cq, qn_ref[...]), wq_ref[...])
    hw = MLA_HEADS * LANES
    q_ref[...] = (q2[:, :hw] * tile(tab[:, :LANES]) + q2[:, hw:] * tile(tab[:, LANES:2 * LANES])).astype(BF16)
    ckvn = _rms(ckv, kvn_ref[...]).astype(BF16)
    kr = kr_a * tab[:, 2 * LANES:3 * LANES] + kr_b * tab[:, 3 * LANES:]
    kt_ref[...] = (jnp.dot(ckvn, wkn_ref[...], preferred_element_type=F32) + tile(kr)).T.astype(BF16)
    v_ref[...] = jnp.dot(ckvn, wv_ref[...], preferred_element_type=F32).astype(BF16)


def _inproj_call(xs, mods_l, g, wm, wr, qn, wq, kvn, wkn, wv, tab, ald, *, batch, seq):
    n = xs.shape[0]
    nblk = n // TM
    nlb = batch * seq // TM
    per_b = seq // TM
    modrow = lambda i: jnp.where(i < nlb, i // per_b, batch)
    tabrow = lambda i: jnp.where(i < nlb, i % per_b, per_b)
    full = lambda a: pl.BlockSpec(a.shape, lambda i: (0,) * a.ndim)
    rowblk = lambda w: pl.BlockSpec((TM, w), lambda i: (i, 0))
    hw = MLA_HEADS * LANES
    return pl.pallas_call(
        _inproj_kernel,
        out_shape=(jax.ShapeDtypeStruct((n, MAIN_COLS), F32),
                   jax.ShapeDtypeStruct((n, LANES), F32),
                   jax.ShapeDtypeStruct((n, hw), BF16),
                   jax.ShapeDtypeStruct((hw, n), BF16),
                   jax.ShapeDtypeStruct((n, MLA_WIDTH), BF16)),
        grid=(nblk,),
        in_specs=[rowblk(D_MODEL),
                  pl.BlockSpec((None, 6, D_MODEL), lambda i: (modrow(i), 0, 0)),
                  full(g), full(wm), full(wr), full(qn), full(wq), full(kvn), full(wkn), full(wv),
                  pl.BlockSpec((TM, 4 * LANES), lambda i: (tabrow(i), 0)), full(ald)],
        out_specs=(rowblk(MAIN_COLS), rowblk(LANES), rowblk(hw), pl.BlockSpec((hw, TM), lambda i: (0, i)),
                   rowblk(MLA_WIDTH)),
        compiler_params=_params("arbitrary"),
        name="inproj",
    )(xs, mods_l, g, wm, wr, qn, wq, kvn, wkn, wv, tab, ald)


def _conv_chunk(src_ref, r, n_chunks, n_rows, w, rows=CONV_ROWS):
    start = pl.multiple_of(r * rows, rows)
    cur = src_ref[pl.ds(start, rows), :]
    prev_start = pl.multiple_of(jnp.maximum(start - SUBLANES, 0), SUBLANES)
    prev = jnp.where(r > 0, src_ref[pl.ds(prev_start, SUBLANES), :], 0.0)
    next_start = pl.multiple_of(jnp.minimum(start + rows, n_rows - SUBLANES), SUBLANES)
    nxt = jnp.where(r < n_chunks - 1, src_ref[pl.ds(next_start, SUBLANES), :], 0.0)
    cat = jnp.concatenate([prev, cur, nxt], axis=0)
    tot = rows + 2 * SUBLANES
    sl = slice(SUBLANES, SUBLANES + rows)
    acc = w[2:3, :] * cur
    acc = acc + w[0:1, :] * pltpu.roll(cat, 2, axis=0)[sl]
    acc = acc + w[1:2, :] * pltpu.roll(cat, 1, axis=0)[sl]
    acc = acc + w[3:4, :] * pltpu.roll(cat, tot - 1, axis=0)[sl]
    return start, acc


GDN_UNROLL = 8


def _gdn_kernel(ql_ref, kl_ref, vl_ref, qc_ref, kc_ref, vc_ref, zl_ref, zc_ref, ggl_ref, ggc_ref,
                cwq_ref, cwk_ref, cwv_ref, nw_ref, yl_ref, yc_ref,
                qs, ks, vs, gs, o_loc, o_seq, lhs_r, upd_r, gl_r, *, seq, ctx):
    head = pl.program_id(1)
    C = GDN_CHUNK
    C2 = 2 * C
    hd = GDN_HEAD_DIM
    U = GDN_UNROLL
    n_ctx = ctx // C
    n_tot = (ctx + seq) // C
    pad = U - n_ctx
    n_groups = (n_tot - n_ctx) // U

    cws = (cwq_ref[...], cwk_ref[...], cwv_ref[...])

    def conv_block(srcs, j, n_rows, off):
        for src, dst, w, kind in zip(srcs, (qs, ks, vs), cws, "qkv"):
            start, acc = _conv_chunk(src, j, n_rows // CONV_ROWS, n_rows, w)
            y = jax.nn.silu(acc)
            if kind != "v":
                y = y * lax.rsqrt(jnp.sum(y * y, axis=-1, keepdims=True) + EPS)
            if kind == "q":
                y = y * (hd ** -0.5)
            dst[pl.ds(off + start, CONV_ROWS), :] = y

    def prep_gates(src_ref, off, n_rows):
        def body(r, carry):
            start = pl.multiple_of(r * CONV_ROWS, CONV_ROWS)
            gs[pl.ds(off + start, CONV_ROWS), :] = src_ref[pl.ds(start, CONV_ROWS), :]
            return carry

        lax.fori_loop(0, n_rows // CONV_ROWS, body, 0)

    prep_gates(ggc_ref, 0, ctx)
    prep_gates(ggl_ref, ctx, seq)
    lat_srcs = (ql_ref, kl_ref, vl_ref)
    n_blk = seq // CONV_ROWS
    per_side = U * C // CONV_ROWS
    for j in range(ctx // CONV_ROWS):
        conv_block((qc_ref, kc_ref, vc_ref), j, ctx, 0)
    for j in sorted(set(range(min(per_side, n_blk))) | set(range(max(n_blk - per_side, 0), n_blk))):
        conv_block(lat_srcs, j, seq, ctx)
    conv_groups = max(n_blk // (2 * per_side) - 1, 0)

    lane = lax.broadcasted_iota(jnp.int32, (C, LANES), 1)
    sub_t = lax.broadcasted_iota(jnp.int32, (LANES, C), 0)
    bwd_lane = (lane % (2 * GDN_HEADS)) >= GDN_HEADS
    prow = lax.broadcasted_iota(jnp.int32, (C2, C2), 0)
    pcol = lax.broadcasted_iota(jnp.int32, (C2, C2), 1)
    sgn = 1 - 2 * (prow // C)
    incl_p = ((prow // C) == (pcol // C)) & ((prow - pcol) * sgn >= 0)
    eye_p = (prow == pcol).astype(F32)
    off_diag = []
    g = 1
    while g < C:
        half = (prow % (2 * g)) // g - (pcol % (2 * g)) // g
        off_diag.append(((prow // (2 * g)) == (pcol // (2 * g))) & (half * sgn == 1))
        g *= 2

    def step_chunks(s):
        return s, jnp.where(s < n_ctx, n_ctx - 1 - s, n_tot - 1 - (s - n_ctx))

    def chunk_rows(c):
        return pl.ds(pl.multiple_of(c * C, C), C)

    def seq_step(entry, v, carry):
        cf, cb = step_chunks(v - pad)
        chunks = (jnp.where(v < pad, n_tot, cf), jnp.where(v < pad, n_tot, cb))
        new = []
        for d in range(2):
            res = jnp.dot(lhs_r[entry, d], carry[d].astype(BF16), preferred_element_type=F32)
            o_seq[d, chunk_rows(chunks[d]), :] = res[:C]
            new.append(carry[d] * gl_r[entry, d, 0:1, :] + upd_r[entry, d] - res[C:])
        return tuple(new)

    def local_steps(steps, entries, tick, after_loads=lambda: None):
        U_ = range(len(steps))
        q2, k2, v2, g_col, b_col, g_last, decay, rows = [], [], [], [], [], [], [], []
        for s in steps:
            cf, cb = step_chunks(s)
            rf, rb = chunk_rows(cf), chunk_rows(cb)
            rows.append((rf, rb))
            q2.append(jnp.concatenate([qs[rf, :], qs[rb, :]], axis=0))
            k2.append(jnp.concatenate([ks[rf, :], ks[rb, :]], axis=0))
            v2.append(jnp.concatenate([vs[rf, :], vs[rb, :]], axis=0))
            gg = jnp.where(bwd_lane, gs[rb, :], gs[rf, :])
            gg_t = gg.T
            g_cols, b_cols, g_rows = [], [], []
            for d in range(2):
                g_lane = 2 * GDN_HEADS + d * GDN_HEADS + head
                g_cols.append(jnp.sum(jnp.where(lane == g_lane, gg, 0.0), axis=-1, keepdims=True))
                b_cols.append(jnp.sum(jnp.where(lane == d * GDN_HEADS + head, gg, 0.0), axis=-1, keepdims=True))
                g_rows.append(jnp.sum(jnp.where(sub_t == g_lane, gg_t, 0.0), axis=0, keepdims=True))
            g_last.append((g_cols[0][C - 1:C, :], g_cols[1][0:1, :]))
            g_col.append(jnp.concatenate(g_cols, axis=0))
            b_col.append(jnp.concatenate(b_cols, axis=0))
            g_row = jnp.concatenate(g_rows, axis=1)
            decay.append(jnp.exp(jnp.where(incl_p, g_col[-1] - g_row, -jnp.inf)))
        after_loads()
        tick()
        r_qk = [lax.dot_general(jnp.concatenate([q2[u], k2[u]], axis=0).astype(BF16), k2[u].astype(BF16), _NT,
                                preferred_element_type=F32) for u in U_]
        attn = [r_qk[u][:C2] * decay[u] for u in U_]
        low = [r_qk[u][C2:] * b_col[u] * decay[u] for u in U_]
        tick()
        t_inv = [eye_p - jnp.where(off_diag[0], low[u], 0.0) for u in U_]
        for m in off_diag[1:]:
            xt = [_bdot(jnp.where(m, low[u], 0.0), t_inv[u]) for u in U_]
            t_inv = [t_inv[u] - _bdot(t_inv[u], xt[u]) for u in U_]
            tick()
        e_g = [jnp.exp(g_col[u]) for u in U_]
        x = [_bdot(t_inv[u], jnp.concatenate([k2[u] * (e_g[u] * b_col[u]), v2[u] * b_col[u]], axis=1))
             for u in U_]
        tick()
        ax = [_bdot(attn[u], x[u]) for u in U_]
        tick()
        zero = jnp.zeros((C, hd), F32)
        kx = []
        for u in U_:
            gl_col = jnp.concatenate([jnp.broadcast_to(g_last[u][0], (C, 1)),
                                      jnp.broadcast_to(g_last[u][1], (C, 1))], axis=0)
            kg = k2[u] * jnp.exp(gl_col - g_col[u])
            kg_blk = jnp.concatenate([jnp.concatenate([kg[:C], zero], axis=1),
                                      jnp.concatenate([zero, kg[C:]], axis=1)], axis=0)
            kx.append(lax.dot_general(kg_blk.astype(BF16), x[u].astype(BF16), _TN,
                                      preferred_element_type=F32))
        tick()
        for u in U_:
            q_eff = q2[u] * e_g[u] - ax[u][:, :hd]
            for d in range(2):
                lhs_r[entries[u], d, 0:C, :] = q_eff[d * C:(d + 1) * C].astype(BF16)
                lhs_r[entries[u], d, C:C + hd, :] = kx[u][d * hd:(d + 1) * hd, :hd].astype(BF16)
                upd_r[entries[u], d] = kx[u][d * hd:(d + 1) * hd, hd:]
                gl_r[entries[u], d] = jnp.broadcast_to(jnp.exp(g_last[u][d]), (SUBLANES, LANES))
                o_loc[d, rows[u][d], :] = ax[u][d * C:(d + 1) * C, hd:]

    for e in range(pad):
        for d in range(2):
            lhs_r[e, d] = jnp.zeros((C + hd, LANES), BF16)
            upd_r[e, d] = jnp.zeros((hd, LANES), F32)
            gl_r[e, d] = jnp.ones((SUBLANES, LANES), F32)
    local_steps(list(range(n_ctx)), [pad + u for u in range(n_ctx)], lambda: None)

    def group(i, carry, with_conv):
        state = {"carry": carry, "next": 0}

        def conv_next():
            for t in range(per_side):
                conv_block(lat_srcs, i * per_side + t, seq, ctx)
                conv_block(lat_srcs, n_blk - 1 - i * per_side - t, seq, ctx)

        prev_base = ((i - 1) % 2) * U

        def tick():
            if state["next"] < U:
                u = state["next"]
                state["carry"] = seq_step(prev_base + u, (i - 1) * U + u, state["carry"])
                state["next"] = u + 1

        base = (i % 2) * U
        local_steps([i * U + u - pad for u in range(U)], [base + u for u in range(U)], tick,
                    conv_next if with_conv else (lambda: None))
        while state["next"] < U:
            tick()
        return state["carry"]

    zero = jnp.zeros((hd, hd), F32)
    carry = lax.fori_loop(1, 1 + conv_groups, functools.partial(group, with_conv=True), (zero, zero))
    carry = lax.fori_loop(1 + conv_groups, n_groups + 1, functools.partial(group, with_conv=False), carry)
    def finish(z_ref, y_ref, off, j):
        start = j * CONV_ROWS
        rows = pl.ds(off + start, CONV_ROWS)
        o = (o_loc[0, rows, :] + o_loc[1, rows, :]) + (o_seq[0, rows, :] + o_seq[1, rows, :])
        y = _rms(o, nw_ref[...]) * jax.nn.silu(z_ref[pl.ds(start, CONV_ROWS), :])
        y_ref[pl.ds(start, CONV_ROWS), :] = y.astype(y_ref.dtype)

    touched = set(range(min(per_side, n_blk))) | set(range(max(n_blk - per_side, 0), n_blk))
    segs = {"ctx": (zc_ref, yc_ref, 0), "lat": (zl_ref, yl_ref, ctx)}
    every = [("ctx", j) for j in range(ctx // CONV_ROWS)] + [("lat", j) for j in range(n_blk)]
    early = [key for key in every if n_groups > 0 and (key[0] == "ctx" or key[1] not in touched)]
    for u in range(U):
        carry = seq_step((n_groups % 2) * U + u, n_groups * U + u, carry)
        for seg, j in early[u::U]:
            finish(*segs[seg], j)
    for seg, j in every:
        if (seg, j) not in early:
            finish(*segs[seg], j)


def _gdn_call(p, gg, conv_w, norm_w, *, batch, seq, ctx):
    H = GDN_HEADS
    C = GDN_CHUNK
    cb0 = batch * seq // ctx
    lat = lambda cblk: pl.BlockSpec((seq, LANES), lambda b, h: (b, cblk * H + h))
    cx = lambda cblk: pl.BlockSpec((ctx, LANES), lambda b, h: (cb0 + b, cblk * H + h))
    cw = lambda cblk: pl.BlockSpec((CONV_WIDTH, LANES), lambda b, h: (0, cblk * H + h))
    T = seq + ctx
    assert ctx // C <= GDN_UNROLL and (seq // C) % GDN_UNROLL == 0
    return pl.pallas_call(
        functools.partial(_gdn_kernel, seq=seq, ctx=ctx),
        out_shape=(jax.ShapeDtypeStruct((batch * seq, GDN_WIDTH), BF16),
                   jax.ShapeDtypeStruct((batch * ctx, GDN_WIDTH), BF16)),
        grid=(batch, H),
        in_specs=[lat(0), lat(1), lat(2), cx(0), cx(1), cx(2), lat(3), cx(3),
                  pl.BlockSpec((seq, LANES), lambda b, h: (b, 0)),
                  pl.BlockSpec((ctx, LANES), lambda b, h: (cb0 + b, 0)),
                  cw(0), cw(1), cw(2),
                  pl.BlockSpec((1, LANES), lambda b, h: (0, 0))],
        out_specs=(pl.BlockSpec((seq, LANES), lambda b, h: (b, h)),
                   pl.BlockSpec((ctx, LANES), lambda b, h: (b, h))),
        scratch_shapes=[pltpu.VMEM((T, LANES), F32)] * 4
                       + [pltpu.VMEM((2, T, LANES), F32), pltpu.VMEM((2, T + C, LANES), F32),
                          pltpu.VMEM((2 * GDN_UNROLL, 2, C + GDN_HEAD_DIM, LANES), BF16),
                          pltpu.VMEM((2 * GDN_UNROLL, 2, GDN_HEAD_DIM, LANES), F32),
                          pltpu.VMEM((2 * GDN_UNROLL, 2, SUBLANES, LANES), F32)],
        compiler_params=_params("arbitrary", "arbitrary"),
        name="gdn",
    )(p, p, p, p, p, p, p, p, gg, gg, conv_w, conv_w, conv_w, norm_w)


def _lru_scan(a, u, fwd):
    n = a.shape[0]
    row = lax.broadcasted_iota(jnp.int32, a.shape, 0)
    s = 1
    while s < n:
        shift = s if fwd else n - s
        valid = (row >= s) if fwd else (row < n - s)
        u = jnp.where(valid, a * pltpu.roll(u, shift, axis=0) + u, u)
        a = jnp.where(valid, a * pltpu.roll(a, shift, axis=0), a)
        s *= 2
    return a, u


def _lru_kernel(xl_ref, xc_ref, gl_ref, gc_ref, cw_ref, cb_ref, w_ref, b_ref, lam_ref, yl_ref, yc_ref,
                ab_s, ub_s, hf_s, *, seq, ctx):
    W = LRU_WIDTH
    neg_log_base = jax.nn.softplus(-lam_ref[...])
    cw = cw_ref[...]

    def forward(src_ref, off, n_rows, h):
        n_chunks = n_rows // CONV_ROWS

        def body(r, h):
            start, acc = _conv_chunk(src_ref, r, n_chunks, n_rows, cw)
            xc = acc + cb_ref[...]
            gates = jax.nn.sigmoid(_bdot(xc, w_ref[...]) + b_ref[...])
            coef = []
            for d in range(2):
                rg = gates[:, 2 * d * W:(2 * d + 1) * W]
                ig = gates[:, (2 * d + 1) * W:(2 * d + 2) * W]
                log_a = -LRU_C * rg * neg_log_base[d:d + 1, :]
                a = jnp.exp(log_a)
                coef.append((a, jnp.sqrt(-jnp.tanh(log_a) * (a * a + 1.0)) * (ig * xc)))
            rows = pl.ds(off + start, CONV_ROWS)
            ab_s[rows, :] = coef[1][0]
            ub_s[rows, :] = coef[1][1]
            a_cum, h0 = _lru_scan(coef[0][0], coef[0][1], True)
            hf = h0 + a_cum * h
            hf_s[rows, :] = hf
            return hf[CONV_ROWS - 1:CONV_ROWS, :]

        return lax.fori_loop(0, n_chunks, body, h)

    def backward(gate_ref, y_ref, off, n_rows, h):
        n_chunks = n_rows // CONV_ROWS

        def body(i, h):
            r = n_chunks - 1 - i
            start = pl.multiple_of(r * CONV_ROWS, CONV_ROWS)
            rows = pl.ds(off + start, CONV_ROWS)
            a_cum, h0 = _lru_scan(ab_s[rows, :], ub_s[rows, :], False)
            hb = h0 + a_cum * h
            y = (hf_s[rows, :] + hb) * jax.nn.gelu(gate_ref[pl.ds(start, CONV_ROWS), :])
            y_ref[pl.ds(start, CONV_ROWS), :] = y.astype(y_ref.dtype)
            return hb[0:1, :]

        return lax.fori_loop(0, n_chunks, body, h)

    zero = jnp.zeros((1, W), F32)
    forward(xl_ref, ctx, seq, forward(xc_ref, 0, ctx, zero))
    backward(gl_ref, yl_ref, ctx, seq, backward(gc_ref, yc_ref, 0, ctx, zero))


def _lru_call(p, conv_w, conv_b, w_blk, b_blk, lam, *, batch, seq, ctx):
    cb0 = batch * seq // ctx
    x_col = (3 * GDN_WIDTH + GDN_WIDTH) // LRU_WIDTH
    T = seq + ctx
    full = lambda a: pl.BlockSpec(a.shape, lambda b: (0,) * a.ndim)
    return pl.pallas_call(
        functools.partial(_lru_kernel, seq=seq, ctx=ctx),
        out_shape=(jax.ShapeDtypeStruct((batch * seq, LRU_WIDTH), BF16),
                   jax.ShapeDtypeStruct((batch * ctx, LRU_WIDTH), BF16)),
        grid=(batch,),
        in_specs=[pl.BlockSpec((seq, LRU_WIDTH), lambda b: (b, x_col)),
                  pl.BlockSpec((ctx, LRU_WIDTH), lambda b: (cb0 + b, x_col)),
                  pl.BlockSpec((seq, LRU_WIDTH), lambda b: (b, x_col + 1)),
                  pl.BlockSpec((ctx, LRU_WIDTH), lambda b: (cb0 + b, x_col + 1)),
                  full(conv_w), full(conv_b), full(w_blk), full(b_blk), full(lam)],
        out_specs=(pl.BlockSpec((seq, LRU_WIDTH), lambda b: (b, 0)),
                   pl.BlockSpec((ctx, LRU_WIDTH), lambda b: (b, 0))),
        scratch_shapes=[pltpu.VMEM((T, LRU_WIDTH), F32)] * 3,
        compiler_params=_params("arbitrary"),
        name="rglru",
    )(p, p, p, p, conv_w, conv_b, w_blk, b_blk, lam)


def _attn_kernel(*refs, n_src):
    q_ref = refs[0]
    srcs = [(refs[1 + 2 * i], refs[2 + 2 * i]) for i in range(n_src)]
    o_ref, s_scr, p_scr = refs[1 + 2 * n_src:]
    tq = q_ref.shape[0]
    lane = lax.broadcasted_iota(jnp.int32, (tq, LANES), 1)
    per_vreg = LANES // MLA_V
    col_blocks = []
    off = 0
    for i, (kt_ref, _) in enumerate(srcs):
        n = kt_ref.shape[1]
        for st in range(0, n, ATT_KB):
            col_blocks.append((i, st, off + st, min(ATT_KB, n - st)))
        off += n

    def fold(acc, x, op):
        for i in range(x.shape[1] // LANES):
            acc = op(acc, x[:, i * LANES:(i + 1) * LANES])
        return acc

    def scores(h):
        hrows = slice(h * LANES, (h + 1) * LANES)
        qh = q_ref[:, hrows]
        m_part = jnp.full((tq, LANES), -jnp.inf, F32)
        for i, st, dst, w in col_blocks:
            s = jnp.dot(qh, srcs[i][0][hrows, st:st + w], preferred_element_type=F32)
            s_scr[h % 2, :, dst:dst + w] = s
            m_part = fold(m_part, s, jnp.maximum)
        return jnp.max(m_part, axis=-1, keepdims=True)

    def probs(h, m):
        l_part = jnp.zeros((tq, LANES), F32)
        for _, _, dst, w in col_blocks:
            p = jnp.exp2(s_scr[h % 2, :, dst:dst + w] - m)
            p_scr[h % 2, :, dst:dst + w] = p.astype(BF16)
            l_part = fold(l_part, p, jnp.add)
        return jnp.sum(l_part, axis=-1, keepdims=True)

    def weighted(h, l_sum):
        vcols = slice((h // per_vreg) * LANES, (h // per_vreg + 1) * LANES)
        acc = None
        off = 0
        for _, v_ref in srcs:
            n = v_ref.shape[0]
            part = jnp.dot(p_scr[h % 2, :, off:off + n], v_ref[:, vcols], preferred_element_type=F32)
            acc = part if acc is None else acc + part
            off += n
        return acc * (1.0 / l_sum)

    outs = []
    m_next = scores(0)
    for h in range(MLA_HEADS):
        m = m_next
        if h + 1 < MLA_HEADS:
            m_next = scores(h + 1)
        outs.append(weighted(h, probs(h, m)))
    tiles = []
    for t in range(MLA_HEADS // per_vreg):
        tile = outs[t * per_vreg]
        for j in range(1, per_vreg):
            tile = jnp.where(lane >= j * MLA_V, outs[t * per_vreg + j], tile)
        tiles.append(tile)
    o_ref[...] = jnp.concatenate(tiles, axis=1).astype(o_ref.dtype)


def _attn_call(q, kt, v, *, batch, seq, ctx, latent):
    hw = MLA_HEADS * LANES
    cb0 = batch * seq // ctx
    kv_specs = [pl.BlockSpec((hw, ctx), lambda b, i: (0, cb0 + b)),
                pl.BlockSpec((ctx, MLA_WIDTH), lambda b, i: (cb0 + b, 0))]
    args = (q, kt, v)
    if latent:
        per_b = seq // ATT_TQ
        grid = (batch, per_b)
        q_spec = pl.BlockSpec((ATT_TQ, hw), lambda b, i: (b * per_b + i, 0))
        kv_specs += [pl.BlockSpec((hw, seq), lambda b, i: (0, b)),
                     pl.BlockSpec((seq, MLA_WIDTH), lambda b, i: (b, 0))]
        args += (kt, v)
        out_spec = pl.BlockSpec((ATT_TQ, MLA_WIDTH), lambda b, i: (b * per_b + i, 0))
        rows, tq, keys = batch * seq, ATT_TQ, ctx + seq
    else:
        grid = (batch, 1)
        q_spec = pl.BlockSpec((ctx, hw), lambda b, i: (cb0 + b, 0))
        out_spec = pl.BlockSpec((ctx, MLA_WIDTH), lambda b, i: (b, 0))
        rows, tq, keys = batch * ctx, ctx, ctx
    return pl.pallas_call(
        functools.partial(_attn_kernel, n_src=len(kv_specs) // 2),
        out_shape=jax.ShapeDtypeStruct((rows, MLA_WIDTH), BF16),
        grid=grid,
        in_specs=[q_spec] + kv_specs,
        out_specs=out_spec,
        scratch_shapes=[pltpu.VMEM((2, tq, keys), F32), pltpu.VMEM((2, tq, keys), BF16)],
        compiler_params=_params("arbitrary", "arbitrary"),
        name="mla_lat" if latent else "mla_ctx",
    )(*args)


def _outmlp_kernel(*refs, with_ctx, n_lat_blocks):
    x_ref, mod_ref = refs[0], refs[1]
    n_y = 6 if with_ctx else 3
    y_refs = refs[2:2 + n_y]
    gpost_ref, gpre_ref, gmlp_ref, wo_ref, w1_ref, w2_ref, o_ref = refs[2 + n_y:]
    if with_ctx:
        is_lat = pl.program_id(0) < n_lat_blocks
        parts = [jnp.where(is_lat, y_refs[2 * i][...], y_refs[2 * i + 1][...]) for i in range(3)]
    else:
        parts = [r[...] for r in y_refs]
    y = jnp.concatenate(parts, axis=1)
    t = jnp.dot(y, wo_ref[...], preferred_element_type=F32)
    x1 = x_ref[...] + mod_ref[2:3, :] * _rms(t, gpost_ref[...])
    h2 = (_rms(x1, gpre_ref[...]) * (1.0 + mod_ref[4:5, :]) + mod_ref[3:4, :]).astype(BF16)
    acc = jnp.zeros(x1.shape, F32)
    for f in range(D_FF // FF_CHUNK):
        cols = slice(f * FF_CHUNK, (f + 1) * FF_CHUNK)
        a = jnp.dot(h2, w1_ref[:, cols], preferred_element_type=F32)
        a = jnp.square(jnp.maximum(a, 0.0)).astype(BF16)
        acc = acc + jnp.dot(a, w2_ref[cols, :], preferred_element_type=F32)
    o_ref[...] = x1 + mod_ref[5:6, :] * _rms(acc, gmlp_ref[...])


def _outmlp_call(xs, mods_l, ys, gpost, gpre, gmlp, wo, w1, w2, *, batch, seq, with_ctx):
    n = xs.shape[0]
    nlb = batch * seq // TM
    nblk = n // TM if with_ctx else nlb
    per_b = seq // TM
    modrow = lambda i: jnp.where(i < nlb, i // per_b, batch)
    rowblk = lambda w: pl.BlockSpec((TM, w), lambda i: (i, 0))
    resident = lambda a: pl.BlockSpec(a.shape, lambda i: (0,) * a.ndim, pipeline_mode=pl.Buffered(1))
    y_specs, y_args = [], []
    for y_lat, y_ctx in ys:
        w = y_lat.shape[1]
        y_specs.append(pl.BlockSpec((TM, w), lambda i: (jnp.minimum(i, nlb - 1), 0)))
        y_args.append(y_lat)
        if with_ctx:
            y_specs.append(pl.BlockSpec((TM, w), lambda i: (jnp.maximum(i - nlb, 0), 0)))
            y_args.append(y_ctx)
    return pl.pallas_call(
        functools.partial(_outmlp_kernel, with_ctx=with_ctx, n_lat_blocks=nlb),
        out_shape=jax.ShapeDtypeStruct((nblk * TM, D_MODEL), F32),
        grid=(nblk,),
        in_specs=[rowblk(D_MODEL), pl.BlockSpec((None, 6, D_MODEL), lambda i: (modrow(i), 0, 0))] + y_specs
                 + [resident(gpost), resident(gpre), resident(gmlp), resident(wo), resident(w1), resident(w2)],
        out_specs=rowblk(D_MODEL),
        compiler_params=_params("arbitrary"),
        name="outproj_mlp",
    )(xs, mods_l, *y_args, gpost, gpre, gmlp, wo, w1, w2)


def _rope_rot_cols(w):
    w4 = w.reshape(w.shape[:-1] + (2, 2, MLA_ROPE // 4))
    return jnp.stack([-w4[..., 1, :], w4[..., 0, :]], axis=-2).reshape(w.shape)


def _zeros_like_cols(w, n):
    return jnp.zeros(w.shape[:-1] + (n,), w.dtype)


def _layout_w_in(w_in):
    o = np.concatenate([[0], np.cumsum(IN_SIZES)])
    qkv, z, beta, dec, lx, lg, cq, ckv, kr = [w_in[..., o[i]:o[i + 1]] for i in range(len(IN_SIZES))]
    zc = functools.partial(_zeros_like_cols, w_in)
    main = jnp.concatenate([qkv, z, lx, lg], axis=-1)
    pad = LANES - MLA_NOPE - MLA_ROPE
    rest = jnp.concatenate([cq, ckv, beta, dec, zc(LANES - 4 * GDN_HEADS),
                            zc(MLA_NOPE), kr, zc(pad), zc(MLA_NOPE), _rope_rot_cols(kr), zc(pad)], axis=-1)
    return main.astype(BF16), rest.astype(BF16)


def _layout_mla(w_uq, w_ukv):
    lead = w_uq.shape[:-1]
    q4 = w_uq.reshape(lead + (MLA_HEADS, MLA_NOPE + MLA_ROPE))
    nope, rope = q4[..., :MLA_NOPE], q4[..., MLA_NOPE:]
    pad = LANES - MLA_NOPE - MLA_ROPE
    wa = jnp.concatenate([nope, rope, _zeros_like_cols(rope, pad)], axis=-1)
    wb = jnp.concatenate([jnp.zeros_like(nope), _rope_rot_cols(rope), _zeros_like_cols(rope, pad)], axis=-1)
    wq = jnp.concatenate([wa.reshape(lead + (-1,)), wb.reshape(lead + (-1,))], axis=-1)
    lead = w_ukv.shape[:-1]
    kv4 = w_ukv.reshape(lead + (MLA_HEADS, MLA_NOPE + MLA_V))
    k_nope, v = kv4[..., :MLA_NOPE], kv4[..., MLA_NOPE:]
    wkn = jnp.concatenate([k_nope, _zeros_like_cols(k_nope, LANES - MLA_NOPE)], axis=-1).reshape(lead + (-1,))
    return wq.astype(BF16), wkn.astype(BF16), v.reshape(lead + (-1,)).astype(BF16)


def _rope_table(seq):
    rows = seq // GRID_W
    row = jnp.repeat(jnp.arange(rows, dtype=F32), GRID_W)
    col = jnp.tile(jnp.arange(GRID_W, dtype=F32), rows)
    half = MLA_ROPE // 2
    inv = ROPE_BASE ** (-jnp.arange(0, half, 2, dtype=F32) / half)
    ang = jnp.stack([row[:, None] * inv, col[:, None] * inv], axis=1)
    ang = jnp.concatenate([ang, ang], axis=-1).reshape(seq, MLA_ROPE)
    cos = jnp.concatenate([jnp.cos(ang), jnp.ones((TM, MLA_ROPE), F32)], axis=0)
    sin = jnp.concatenate([jnp.sin(ang), jnp.zeros((TM, MLA_ROPE), F32)], axis=0)
    n = seq + TM
    one, zero = jnp.ones((n, MLA_NOPE), F32), jnp.zeros((n, MLA_NOPE), F32)
    pad = jnp.zeros((n, LANES - MLA_NOPE - MLA_ROPE), F32)
    scale = (MLA_NOPE + MLA_ROPE) ** -0.5 * math.log2(math.e)
    return jnp.concatenate([scale * one, scale * cos, pad, zero, scale * sin, pad,
                            zero, cos, pad, zero, sin, pad], axis=1)


def _layout_lru(w_a, b_a, w_i, b_i):
    def blockdiag(w):
        g, n = w.shape[1], w.shape[2]
        out = jnp.zeros((w.shape[0], g * n, g * n), w.dtype)
        for j in range(g):
            out = out.at[:, j * n:(j + 1) * n, j * n:(j + 1) * n].set(w[:, j])
        return out

    da, di = blockdiag(w_a), blockdiag(w_i)
    w = jnp.concatenate([da[0], di[0], da[1], di[1]], axis=-1).astype(BF16)
    b = jnp.concatenate([b_a[0], b_i[0], b_a[1], b_i[1]], axis=-1)[None, :]
    return w, b


def kernel(x, c, ctx, c_ctx, w_ada, b_ada, g_attn_pre, g_attn_post, g_mlp_pre, g_mlp_post, w_in, gdn_conv_w,
           gdn_a_log, gdn_dt_bias, gdn_norm_w, lru_conv_w, lru_conv_b, lru_w_a, lru_b_a, lru_w_i, lru_b_i,
           lru_lambda, mla_q_norm, mla_w_uq, mla_kv_norm, mla_w_ukv, w_out, w_mlp1, w_mlp2):
    batch, seq, _ = x.shape
    n_ctx = ctx.shape[1]
    depth = w_ada.shape[0]
    assert seq % TM == 0 and (batch * n_ctx) % TM == 0 and n_ctx % CONV_ROWS == 0 and seq % GRID_W == 0
    assert (batch * seq) % n_ctx == 0 and batch + 1 <= 2 * SUBLANES
    dims = dict(batch=batch, seq=seq, ctx=n_ctx)

    cond = jnp.concatenate([c, c_ctx[None, :], jnp.zeros((2 * SUBLANES - batch - 1, D_MODEL), F32)], axis=0)
    mods = _ada_call(cond, w_ada, b_ada).reshape(depth, 2 * SUBLANES, 6, D_MODEL)
    tab = _rope_table(seq)
    xs = jnp.concatenate([x.reshape(batch * seq, D_MODEL), ctx.reshape(batch * n_ctx, D_MODEL)], axis=0)
    row = lambda a: a[None, :]

    for l in range(depth):
        last = l == depth - 1
        wm, wr = _layout_w_in(w_in[l])
        wq, wkn, wv = _layout_mla(mla_w_uq[l], mla_w_ukv[l])
        pad = jnp.zeros((LANES - 4 * GDN_HEADS,), F32)
        ald = jnp.stack([jnp.concatenate([jnp.zeros((2 * GDN_HEADS,), F32), gdn_a_log[l].reshape(-1), pad]),
                         jnp.concatenate([jnp.zeros((2 * GDN_HEADS,), F32), gdn_dt_bias[l].reshape(-1), pad])])
        p, gg, q, kt, v = _inproj_call(xs, mods[l], row(g_attn_pre[l]), wm, wr, row(mla_q_norm[l]), wq,
                                      row(mla_kv_norm[l]), wkn, wv, tab, ald, batch=batch, seq=seq)
        y_gdn = _gdn_call(p, gg, gdn_conv_w[l], row(gdn_norm_w[l]), **dims)
        w_blk, b_blk = _layout_lru(lru_w_a[l], lru_b_a[l], lru_w_i[l], lru_b_i[l])
        y_lru = _lru_call(p, lru_conv_w[l], row(lru_conv_b[l]), w_blk, b_blk, lru_lambda[l], **dims)
        y_att = (_attn_call(q, kt, v, latent=True, **dims),
                 None if last else _attn_call(q, kt, v, latent=False, **dims))
        xs = _outmlp_call(xs, mods[l], (y_gdn, y_lru, y_att), row(g_attn_post[l]), row(g_mlp_pre[l]),
                          row(g_mlp_post[l]), w_out[l].astype(BF16), w_mlp1[l].astype(BF16),
                          w_mlp2[l].astype(BF16), batch=batch, seq=seq, with_ctx=not last)
    return xs.reshape(batch, seq, D_MODEL)
```

```python
import functools
import math

import numpy as np
import jax
import jax.numpy as jnp
from jax import lax
from jax.experimental import pallas as pl
from jax.experimental.pallas import tpu as pltpu

F32 = jnp.float32
BF16 = jnp.bfloat16

D_MODEL = 1024
EPS = 1e-6
GRID_W = 64
CONV_WIDTH = 4
GDN_HEAD_DIM = 128
GDN_HEADS = 4
GDN_WIDTH = GDN_HEADS * GDN_HEAD_DIM
GDN_CHUNK = 64
LRU_WIDTH = 256
LRU_BLOCKS = 4
LRU_C = 8.0
MLA_V = 64
MLA_HEADS = 4
MLA_WIDTH = MLA_HEADS * MLA_V
MLA_NOPE = 64
MLA_ROPE = 32
MLA_Q_RANK = 256
MLA_KV_RANK = 128
ROPE_BASE = 10000.0
D_FF = 4 * D_MODEL
IN_SIZES = (3 * GDN_WIDTH, GDN_WIDTH, 2 * GDN_HEADS, 2 * GDN_HEADS, LRU_WIDTH, LRU_WIDTH,
            MLA_Q_RANK, MLA_KV_RANK, MLA_ROPE)

LANES = 128
SUBLANES = 8
TM = 512
MAIN_COLS = 3 * GDN_WIDTH + GDN_WIDTH + 2 * LRU_WIDTH
REST_COLS = MLA_Q_RANK + MLA_KV_RANK + 3 * LANES
CONV_ROWS = 256
ATT_TQ = 512
ATT_KB = 512
FF_CHUNK = 1024
VMEM_LIMIT = 56 * 1024 * 1024

_NT = (((1,), (1,)), ((), ()))
_TN = (((0,), (0,)), ((), ()))


def _rms(x, g):
    return x * lax.rsqrt(jnp.mean(x * x, axis=-1, keepdims=True) + EPS) * g


def _bdot(a, b):
    return jnp.dot(a.astype(BF16), b.astype(BF16), preferred_element_type=F32)


def _token_rows(arr, n_lat_blocks, width):
    if isinstance(arr, tuple):
        return ([pl.BlockSpec((TM, width), lambda i: (jnp.minimum(i, n_lat_blocks - 1), 0)),
                 pl.BlockSpec((TM, width), lambda i: (jnp.maximum(i - n_lat_blocks, 0), 0))], list(arr))
    return [pl.BlockSpec((TM, width), lambda i: (i, 0))], [arr]


def _read_token_rows(refs, n_lat_blocks):
    if len(refs) == 2:
        return jnp.where(pl.program_id(0) < n_lat_blocks, refs[0][...], refs[1][...])
    return refs[0][...]


def _params(*sem):
    return pltpu.CompilerParams(dimension_semantics=sem, vmem_limit_bytes=VMEM_LIMIT)


def _ada_kernel(c_ref, w_ref, b_ref, o_ref):
    o_ref[...] = _bdot(jax.nn.silu(c_ref[...]), w_ref[...]) + b_ref[...]


def _ada_call(cond, w_ada, b_ada):
    depth = w_ada.shape[0]
    rows = cond.shape[0]
    return pl.pallas_call(
        _ada_kernel,
        out_shape=jax.ShapeDtypeStruct((depth, rows, 6 * D_MODEL), F32),
        grid=(depth, 6),
        in_specs=[pl.BlockSpec((rows, D_MODEL), lambda l, j: (0, 0)),
                  pl.BlockSpec((None, D_MODEL, D_MODEL), lambda l, j: (l, 0, j)),
                  pl.BlockSpec((None, 1, D_MODEL), lambda l, j: (l, 0, j))],
        out_specs=pl.BlockSpec((None, rows, D_MODEL), lambda l, j: (l, 0, j)),
        compiler_params=_params("arbitrary", "arbitrary"),
        name="adaln",
    )(cond, w_ada, b_ada.reshape(depth, 1, 6 * D_MODEL))


def _gdn_gates(raw, a_log, dt_b):
    rows = raw.shape[0]
    lane = lax.broadcasted_iota(jnp.int32, raw.shape, 1)
    pos = lax.broadcasted_iota(jnp.int32, raw.shape, 0) % GDN_CHUNK
    g = -jnp.exp(a_log) * jax.nn.softplus(raw + dt_b)
    cum_f, cum_b = g, g
    s = 1
    while s < GDN_CHUNK:
        cum_f = cum_f + jnp.where(pos >= s, pltpu.roll(cum_f, s, axis=0), 0.0)
        cum_b = cum_b + jnp.where(pos < GDN_CHUNK - s, pltpu.roll(cum_b, rows - s, axis=0), 0.0)
        s *= 2
    return jnp.where(lane < 2 * GDN_HEADS, jax.nn.sigmoid(raw), jnp.where(lane < 3 * GDN_HEADS, cum_f, cum_b))


def _inproj_kernel(*refs, n_x, n_lat_blocks):
    (mod_ref, g_ref, wm_ref, wr_ref, qn_ref, wq_ref, kvn_ref, wkn_ref, wv_ref, tab_ref, ald_ref,
     p_ref, gg_ref, q_ref, kt_ref, v_ref) = refs[n_x:]
    h = _rms(_read_token_rows(refs[:n_x], n_lat_blocks), g_ref[...]) * (1.0 + mod_ref[1:2, :]) + mod_ref[0:1, :]
    hb = h.astype(BF16)
    p_ref[...] = jnp.dot(hb, wm_ref[...], preferred_element_type=F32)
    r = jnp.dot(hb, wr_ref[...], preferred_element_type=F32)
    o0 = MLA_Q_RANK
    o1 = o0 + MLA_KV_RANK
    cq = r[:, :o0]
    ckv = r[:, o0:o1]
    gg_ref[...] = _gdn_gates(r[:, o1:o1 + LANES], ald_ref[0:1, :], ald_ref[1:2, :])
    kr_a = r[:, o1 + LANES:o1 + 2 * LANES]
    kr_b = r[:, o1 + 2 * LANES:o1 + 3 * LANES]
    tab = tab_ref[...]
    tile = lambda t: jnp.concatenate([t] * MLA_HEADS, axis=1)
    q2 = _bdot(_rms(cq, qn_ref[...]), wq_ref[...])
    hw = MLA_HEADS * LANES
    q_ref[...] = (q2[:, :hw] * tile(tab[:, :LANES]) + q2[:, hw:] * tile(tab[:, LANES:2 * LANES])).astype(BF16)
    ckvn = _rms(ckv, kvn_ref[...]).astype(BF16)
    kr = kr_a * tab[:, 2 * LANES:3 * LANES] + kr_b * tab[:, 3 * LANES:]
    kt_ref[...] = (jnp.dot(ckvn, wkn_ref[...], preferred_element_type=F32) + tile(kr)).T.astype(BF16)
    v_ref[...] = jnp.dot(ckvn, wv_ref[...], preferred_element_type=F32).astype(BF16)


def _inproj_call(xs, mods_l, g, wm, wr, qn, wq, kvn, wkn, wv, tab, ald, *, batch, seq):
    n = sum(a.shape[0] for a in xs) if isinstance(xs, tuple) else xs.shape[0]
    nblk = n // TM
    nlb = batch * seq // TM
    per_b = seq // TM
    x_specs, x_args = _token_rows(xs, nlb, D_MODEL)
    modrow = lambda i: jnp.where(i < nlb, i // per_b, batch)
    tabrow = lambda i: jnp.where(i < nlb, i % per_b, per_b)
    full = lambda a: pl.BlockSpec(a.shape, lambda i: (0,) * a.ndim)
    rowblk = lambda w: pl.BlockSpec((TM, w), lambda i: (i, 0))
    hw = MLA_HEADS * LANES
    return pl.pallas_call(
        functools.partial(_inproj_kernel, n_x=len(x_args), n_lat_blocks=nlb),
        out_shape=(jax.ShapeDtypeStruct((n, MAIN_COLS), F32),
                   jax.ShapeDtypeStruct((n, LANES), F32),
                   jax.ShapeDtypeStruct((n, hw), BF16),
                   jax.ShapeDtypeStruct((hw, n), BF16),
                   jax.ShapeDtypeStruct((n, MLA_WIDTH), BF16)),
        grid=(nblk,),
        in_specs=x_specs + [
                  pl.BlockSpec((None, 6, D_MODEL), lambda i: (modrow(i), 0, 0)),
                  full(g), full(wm), full(wr), full(qn), full(wq), full(kvn), full(wkn), full(wv),
                  pl.BlockSpec((TM, 4 * LANES), lambda i: (tabrow(i), 0)), full(ald)],
        out_specs=(rowblk(MAIN_COLS), rowblk(LANES), rowblk(hw), pl.BlockSpec((hw, TM), lambda i: (0, i)),
                   rowblk(MLA_WIDTH)),
        compiler_params=_params("arbitrary"),
        name="inproj",
    )(*x_args, mods_l, g, wm, wr, qn, wq, kvn, wkn, wv, tab, ald)


def _conv_chunk(src_ref, r, n_chunks, n_rows, w, rows=CONV_ROWS):
    start = pl.multiple_of(r * rows, rows)
    cur = src_ref[pl.ds(start, rows), :]
    prev_start = pl.multiple_of(jnp.maximum(start - SUBLANES, 0), SUBLANES)
    prev = jnp.where(r > 0, src_ref[pl.ds(prev_start, SUBLANES), :], 0.0)
    next_start = pl.multiple_of(jnp.minimum(start + rows, n_rows - SUBLANES), SUBLANES)
    nxt = jnp.where(r < n_chunks - 1, src_ref[pl.ds(next_start, SUBLANES), :], 0.0)
    cat = jnp.concatenate([prev, cur, nxt], axis=0)
    tot = rows + 2 * SUBLANES
    sl = slice(SUBLANES, SUBLANES + rows)
    acc = w[2:3, :] * cur
    acc = acc + w[0:1, :] * pltpu.roll(cat, 2, axis=0)[sl]
    acc = acc + w[1:2, :] * pltpu.roll(cat, 1, axis=0)[sl]
    acc = acc + w[3:4, :] * pltpu.roll(cat, tot - 1, axis=0)[sl]
    return start, acc


GDN_UNROLL = 8


def _gdn_kernel(ql_ref, kl_ref, vl_ref, qc_ref, kc_ref, vc_ref, zl_ref, zc_ref, ggl_ref, ggc_ref,
                cwq_ref, cwk_ref, cwv_ref, nw_ref, yl_ref, yc_ref,
                qs, ks, vs, gs, o_loc, o_seq, lhs_r, upd_r, gl_r, *, seq, ctx):
    head = pl.program_id(1)
    C = GDN_CHUNK
    C2 = 2 * C
    hd = GDN_HEAD_DIM
    U = GDN_UNROLL
    n_ctx = ctx // C
    n_tot = (ctx + seq) // C
    pad = U - n_ctx
    n_groups = (n_tot - n_ctx) // U

    cws = (cwq_ref[...], cwk_ref[...], cwv_ref[...])

    def conv_block(srcs, j, n_rows, off):
        for src, dst, w, kind in zip(srcs, (qs, ks, vs), cws, "qkv"):
            start, acc = _conv_chunk(src, j, n_rows // CONV_ROWS, n_rows, w)
            y = jax.nn.silu(acc)
            if kind != "v":
                y = y * lax.rsqrt(jnp.sum(y * y, axis=-1, keepdims=True) + EPS)
            if kind == "q":
                y = y * (hd ** -0.5)
            dst[pl.ds(off + start, CONV_ROWS), :] = y

    def prep_gates(src_ref, off, n_rows):
        def body(r, carry):
            start = pl.multiple_of(r * CONV_ROWS, CONV_ROWS)
            gs[pl.ds(off + start, CONV_ROWS), :] = src_ref[pl.ds(start, CONV_ROWS), :]
            return carry

        lax.fori_loop(0, n_rows // CONV_ROWS, body, 0)

    prep_gates(ggc_ref, 0, ctx)
    prep_gates(ggl_ref, ctx, seq)
    lat_srcs = (ql_ref, kl_ref, vl_ref)
    n_blk = seq // CONV_ROWS
    per_side = U * C // CONV_ROWS
    for j in range(ctx // CONV_ROWS):
        conv_block((qc_ref, kc_ref, vc_ref), j, ctx, 0)
    for j in sorted(set(range(min(per_side, n_blk))) | set(range(max(n_blk - per_side, 0), n_blk))):
        conv_block(lat_srcs, j, seq, ctx)
    conv_groups = max(n_blk // (2 * per_side) - 1, 0)

    lane = lax.broadcasted_iota(jnp.int32, (C, LANES), 1)
    sub_t = lax.broadcasted_iota(jnp.int32, (LANES, C), 0)
    bwd_lane = (lane % (2 * GDN_HEADS)) >= GDN_HEADS
    prow = lax.broadcasted_iota(jnp.int32, (C2, C2), 0)
    pcol = lax.broadcasted_iota(jnp.int32, (C2, C2), 1)
    sgn = 1 - 2 * (prow // C)
    incl_p = ((prow // C) == (pcol // C)) & ((prow - pcol) * sgn >= 0)
    eye_p = (prow == pcol).astype(F32)
    off_diag = []
    g = 1
    while g < C:
        half = (prow % (2 * g)) // g - (pcol % (2 * g)) // g
        off_diag.append(((prow // (2 * g)) == (pcol // (2 * g))) & (half * sgn == 1))
        g *= 2

    def step_chunks(s):
        return s, jnp.where(s < n_ctx, n_ctx - 1 - s, n_tot - 1 - (s - n_ctx))

    def chunk_rows(c):
        return pl.ds(pl.multiple_of(c * C, C), C)

    def seq_step(entry, v, carry):
        cf, cb = step_chunks(v - pad)
        chunks = (jnp.where(v < pad, n_tot, cf), jnp.where(v < pad, n_tot, cb))
        new = []
        for d in range(2):
            res = jnp.dot(lhs_r[entry, d], carry[d].astype(BF16), preferred_element_type=F32)
            o_seq[d, chunk_rows(chunks[d]), :] = res[:C]
            new.append(carry[d] * gl_r[entry, d, 0:1, :] + upd_r[entry, d] - res[C:])
        return tuple(new)

    def local_steps(steps, entries, tick, after_loads=lambda: None):
        U_ = range(len(steps))
        q2, k2, v2, g_col, b_col, g_last, decay, rows = [], [], [], [], [], [], [], []
        for s in steps:
            cf, cb = step_chunks(s)
            rf, rb = chunk_rows(cf), chunk_rows(cb)
            rows.append((rf, rb))
            q2.append(jnp.concatenate([qs[rf, :], qs[rb, :]], axis=0))
            k2.append(jnp.concatenate([ks[rf, :], ks[rb, :]], axis=0))
            v2.append(jnp.concatenate([vs[rf, :], vs[rb, :]], axis=0))
            gg = jnp.where(bwd_lane, gs[rb, :], gs[rf, :])
            gg_t = gg.T
            g_cols, b_cols, g_rows = [], [], []
            for d in range(2):
                g_lane = 2 * GDN_HEADS + d * GDN_HEADS + head
                g_cols.append(jnp.sum(jnp.where(lane == g_lane, gg, 0.0), axis=-1, keepdims=True))
                b_cols.append(jnp.sum(jnp.where(lane == d * GDN_HEADS + head, gg, 0.0), axis=-1, keepdims=True))
                g_rows.append(jnp.sum(jnp.where(sub_t == g_lane, gg_t, 0.0), axis=0, keepdims=True))
            g_last.append((g_cols[0][C - 1:C, :], g_cols[1][0:1, :]))
            g_col.append(jnp.concatenate(g_cols, axis=0))
            b_col.append(jnp.concatenate(b_cols, axis=0))
            g_row = jnp.concatenate(g_rows, axis=1)
            decay.append(jnp.exp(jnp.where(incl_p, g_col[-1] - g_row, -jnp.inf)))
        after_loads()
        tick()
        r_qk = [lax.dot_general(jnp.concatenate([q2[u], k2[u]], axis=0).astype(BF16), k2[u].astype(BF16), _NT,
                                preferred_element_type=F32) for u in U_]
        attn = [r_qk[u][:C2] * decay[u] for u in U_]
        low = [r_qk[u][C2:] * b_col[u] * decay[u] for u in U_]
        tick()
        t_inv = [eye_p - jnp.where(off_diag[0], low[u], 0.0) for u in U_]
        for m in off_diag[1:]:
            xt = [_bdot(jnp.where(m, low[u], 0.0), t_inv[u]) for u in U_]
            t_inv = [t_inv[u] - _bdot(t_inv[u], xt[u]) for u in U_]
            tick()
        e_g = [jnp.exp(g_col[u]) for u in U_]
        x = [_bdot(t_inv[u], jnp.concatenate([k2[u] * (e_g[u] * b_col[u]), v2[u] * b_col[u]], axis=1))
             for u in U_]
        tick()
        ax = [_bdot(attn[u], x[u]) for u in U_]
        tick()
        zero = jnp.zeros((C, hd), F32)
        kx = []
        for u in U_:
            gl_col = jnp.concatenate([jnp.broadcast_to(g_last[u][0], (C, 1)),
                                      jnp.broadcast_to(g_last[u][1], (C, 1))], axis=0)
            kg = k2[u] * jnp.exp(gl_col - g_col[u])
            kg_blk = jnp.concatenate([jnp.concatenate([kg[:C], zero], axis=1),
                                      jnp.concatenate([zero, kg[C:]], axis=1)], axis=0)
            kx.append(lax.dot_general(kg_blk.astype(BF16), x[u].astype(BF16), _TN,
                                      preferred_element_type=F32))
        tick()
        for u in U_:
            q_eff = q2[u] * e_g[u] - ax[u][:, :hd]
            for d in range(2):
                lhs_r[entries[u], d, 0:C, :] = q_eff[d * C:(d + 1) * C].astype(BF16)
                lhs_r[entries[u], d, C:C + hd, :] = kx[u][d * hd:(d + 1) * hd, :hd].astype(BF16)
                upd_r[entries[u], d] = kx[u][d * hd:(d + 1) * hd, hd:]
                gl_r[entries[u], d] = jnp.broadcast_to(jnp.exp(g_last[u][d]), (SUBLANES, LANES))
                o_loc[d, rows[u][d], :] = ax[u][d * C:(d + 1) * C, hd:]

    for e in range(pad):
        for d in range(2):
            lhs_r[e, d] = jnp.zeros((C + hd, LANES), BF16)
            upd_r[e, d] = jnp.zeros((hd, LANES), F32)
            gl_r[e, d] = jnp.ones((SUBLANES, LANES), F32)
    local_steps(list(range(n_ctx)), [pad + u for u in range(n_ctx)], lambda: None)

    def group(i, carry, with_conv):
        state = {"carry": carry, "next": 0}

        def conv_next():
            for t in range(per_side):
                conv_block(lat_srcs, i * per_side + t, seq, ctx)
                conv_block(lat_srcs, n_blk - 1 - i * per_side - t, seq, ctx)

        prev_base = ((i - 1) % 2) * U

        def tick():
            if state["next"] < U:
                u = state["next"]
                state["carry"] = seq_step(prev_base + u, (i - 1) * U + u, state["carry"])
                state["next"] = u + 1

        base = (i % 2) * U
        local_steps([i * U + u - pad for u in range(U)], [base + u for u in range(U)], tick,
                    conv_next if with_conv else (lambda: None))
        while state["next"] < U:
            tick()
        return state["carry"]

    zero = jnp.zeros((hd, hd), F32)
    carry = lax.fori_loop(1, 1 + conv_groups, functools.partial(group, with_conv=True), (zero, zero))
    carry = lax.fori_loop(1 + conv_groups, n_groups + 1, functools.partial(group, with_conv=False), carry)
    def finish(z_ref, y_ref, off, j):
        start = j * CONV_ROWS
        rows = pl.ds(off + start, CONV_ROWS)
        o = (o_loc[0, rows, :] + o_loc[1, rows, :]) + (o_seq[0, rows, :] + o_seq[1, rows, :])
        y = _rms(o, nw_ref[...]) * jax.nn.silu(z_ref[pl.ds(start, CONV_ROWS), :])
        y_ref[pl.ds(start, CONV_ROWS), :] = y.astype(y_ref.dtype)

    touched = set(range(min(per_side, n_blk))) | set(range(max(n_blk - per_side, 0), n_blk))
    segs = {"ctx": (zc_ref, yc_ref, 0), "lat": (zl_ref, yl_ref, ctx)}
    every = [("ctx", j) for j in range(ctx // CONV_ROWS)] + [("lat", j) for j in range(n_blk)]
    early = [key for key in every if n_groups > 0 and (key[0] == "ctx" or key[1] not in touched)]
    for u in range(U):
        carry = seq_step((n_groups % 2) * U + u, n_groups * U + u, carry)
        for seg, j in early[u::U]:
            finish(*segs[seg], j)
    for seg, j in every:
        if (seg, j) not in early:
            finish(*segs[seg], j)


def _gdn_call(p, gg, conv_w, norm_w, *, batch, seq, ctx):
    H = GDN_HEADS
    C = GDN_CHUNK
    cb0 = batch * seq // ctx
    lat = lambda cblk: pl.BlockSpec((seq, LANES), lambda b, h: (b, cblk * H + h))
    cx = lambda cblk: pl.BlockSpec((ctx, LANES), lambda b, h: (cb0 + b, cblk * H + h))
    cw = lambda cblk: pl.BlockSpec((CONV_WIDTH, LANES), lambda b, h: (0, cblk * H + h))
    T = seq + ctx
    assert ctx // C <= GDN_UNROLL and (seq // C) % GDN_UNROLL == 0
    return pl.pallas_call(
        functools.partial(_gdn_kernel, seq=seq, ctx=ctx),
        out_shape=(jax.ShapeDtypeStruct((batch * seq, GDN_WIDTH), BF16),
                   jax.ShapeDtypeStruct((batch * ctx, GDN_WIDTH), BF16)),
        grid=(batch, H),
        in_specs=[lat(0), lat(1), lat(2), cx(0), cx(1), cx(2), lat(3), cx(3),
                  pl.BlockSpec((seq, LANES), lambda b, h: (b, 0)),
                  pl.BlockSpec((ctx, LANES), lambda b, h: (cb0 + b, 0)),
                  cw(0), cw(1), cw(2),
                  pl.BlockSpec((1, LANES), lambda b, h: (0, 0))],
        out_specs=(pl.BlockSpec((seq, LANES), lambda b, h: (b, h)),
                   pl.BlockSpec((ctx, LANES), lambda b, h: (b, h))),
        scratch_shapes=[pltpu.VMEM((T, LANES), F32)] * 4
                       + [pltpu.VMEM((2, T, LANES), F32), pltpu.VMEM((2, T + C, LANES), F32),
                          pltpu.VMEM((2 * GDN_UNROLL, 2, C + GDN_HEAD_DIM, LANES), BF16),
                          pltpu.VMEM((2 * GDN_UNROLL, 2, GDN_HEAD_DIM, LANES), F32),
                          pltpu.VMEM((2 * GDN_UNROLL, 2, SUBLANES, LANES), F32)],
        compiler_params=_params("arbitrary", "arbitrary"),
        name="gdn",
    )(p, p, p, p, p, p, p, p, gg, gg, conv_w, conv_w, conv_w, norm_w)


def _lru_scan(a, u, fwd):
    n = a.shape[0]
    row = lax.broadcasted_iota(jnp.int32, a.shape, 0)
    s = 1
    while s < n:
        shift = s if fwd else n - s
        valid = (row >= s) if fwd else (row < n - s)
        u = jnp.where(valid, a * pltpu.roll(u, shift, axis=0) + u, u)
        a = jnp.where(valid, a * pltpu.roll(a, shift, axis=0), a)
        s *= 2
    return a, u


def _lru_kernel(xl_ref, xc_ref, gl_ref, gc_ref, cw_ref, cb_ref, w_ref, b_ref, lam_ref, yl_ref, yc_ref,
                ab_s, ub_s, hf_s, *, seq, ctx):
    W = LRU_WIDTH
    neg_log_base = jax.nn.softplus(-lam_ref[...])
    cw = cw_ref[...]

    def forward(src_ref, off, n_rows, h):
        n_chunks = n_rows // CONV_ROWS

        def body(r, h):
            start, acc = _conv_chunk(src_ref, r, n_chunks, n_rows, cw)
            xc = acc + cb_ref[...]
            gates = jax.nn.sigmoid(_bdot(xc, w_ref[...]) + b_ref[...])
            coef = []
            for d in range(2):
                rg = gates[:, 2 * d * W:(2 * d + 1) * W]
                ig = gates[:, (2 * d + 1) * W:(2 * d + 2) * W]
                log_a = -LRU_C * rg * neg_log_base[d:d + 1, :]
                a = jnp.exp(log_a)
                coef.append((a, jnp.sqrt(-jnp.tanh(log_a) * (a * a + 1.0)) * (ig * xc)))
            rows = pl.ds(off + start, CONV_ROWS)
            ab_s[rows, :] = coef[1][0]
            ub_s[rows, :] = coef[1][1]
            a_cum, h0 = _lru_scan(coef[0][0], coef[0][1], True)
            hf = h0 + a_cum * h
            hf_s[rows, :] = hf
            return hf[CONV_ROWS - 1:CONV_ROWS, :]

        return lax.fori_loop(0, n_chunks, body, h)

    def backward(gate_ref, y_ref, off, n_rows, h):
        n_chunks = n_rows // CONV_ROWS

        def body(i, h):
            r = n_chunks - 1 - i
            start = pl.multiple_of(r * CONV_ROWS, CONV_ROWS)
            rows = pl.ds(off + start, CONV_ROWS)
            a_cum, h0 = _lru_scan(ab_s[rows, :], ub_s[rows, :], False)
            hb = h0 + a_cum * h
            y = (hf_s[rows, :] + hb) * jax.nn.gelu(gate_ref[pl.ds(start, CONV_ROWS), :])
            y_ref[pl.ds(start, CONV_ROWS), :] = y.astype(y_ref.dtype)
            return hb[0:1, :]

        return lax.fori_loop(0, n_chunks, body, h)

    zero = jnp.zeros((1, W), F32)
    forward(xl_ref, ctx, seq, forward(xc_ref, 0, ctx, zero))
    backward(gl_ref, yl_ref, ctx, seq, backward(gc_ref, yc_ref, 0, ctx, zero))


def _lru_call(p, conv_w, conv_b, w_blk, b_blk, lam, *, batch, seq, ctx):
    cb0 = batch * seq // ctx
    x_col = (3 * GDN_WIDTH + GDN_WIDTH) // LRU_WIDTH
    T = seq + ctx
    full = lambda a: pl.BlockSpec(a.shape, lambda b: (0,) * a.ndim)
    return pl.pallas_call(
        functools.partial(_lru_kernel, seq=seq, ctx=ctx),
        out_shape=(jax.ShapeDtypeStruct((batch * seq, LRU_WIDTH), BF16),
                   jax.ShapeDtypeStruct((batch * ctx, LRU_WIDTH), BF16)),
        grid=(batch,),
        in_specs=[pl.BlockSpec((seq, LRU_WIDTH), lambda b: (b, x_col)),
                  pl.BlockSpec((ctx, LRU_WIDTH), lambda b: (cb0 + b, x_col)),
                  pl.BlockSpec((seq, LRU_WIDTH), lambda b: (b, x_col + 1)),
                  pl.BlockSpec((ctx, LRU_WIDTH), lambda b: (cb0 + b, x_col + 1)),
                  full(conv_w), full(conv_b), full(w_blk), full(b_blk), full(lam)],
        out_specs=(pl.BlockSpec((seq, LRU_WIDTH), lambda b: (b, 0)),
                   pl.BlockSpec((ctx, LRU_WIDTH), lambda b: (b, 0))),
        scratch_shapes=[pltpu.VMEM((T, LRU_WIDTH), F32)] * 3,
        compiler_params=_params("arbitrary"),
        name="rglru",
    )(p, p, p, p, conv_w, conv_b, w_blk, b_blk, lam)


def _attn_kernel(*refs, n_src):
    q_ref = refs[0]
    srcs = [(refs[1 + 2 * i], refs[2 + 2 * i]) for i in range(n_src)]
    o_ref, s_scr, p_scr = refs[1 + 2 * n_src:]
    tq = q_ref.shape[0]
    lane = lax.broadcasted_iota(jnp.int32, (tq, LANES), 1)
    per_vreg = LANES // MLA_V
    col_blocks = []
    off = 0
    for i, (kt_ref, _) in enumerate(srcs):
        n = kt_ref.shape[1]
        for st in range(0, n, ATT_KB):
            col_blocks.append((i, st, off + st, min(ATT_KB, n - st)))
        off += n

    def fold(acc, x, op):
        for i in range(x.shape[1] // LANES):
            acc = op(acc, x[:, i * LANES:(i + 1) * LANES])
        return acc

    def scores(h):
        hrows = slice(h * LANES, (h + 1) * LANES)
        qh = q_ref[:, hrows]
        m_part = jnp.full((tq, LANES), -jnp.inf, F32)
        for i, st, dst, w in col_blocks:
            s = jnp.dot(qh, srcs[i][0][hrows, st:st + w], preferred_element_type=F32)
            s_scr[h % 2, :, dst:dst + w] = s
            m_part = fold(m_part, s, jnp.maximum)
        return jnp.max(m_part, axis=-1, keepdims=True)

    def probs(h, m):
        l_part = jnp.zeros((tq, LANES), F32)
        for _, _, dst, w in col_blocks:
            p = jnp.exp2(s_scr[h % 2, :, dst:dst + w] - m)
            p_scr[h % 2, :, dst:dst + w] = p.astype(BF16)
            l_part = fold(l_part, p, jnp.add)
        return jnp.sum(l_part, axis=-1, keepdims=True)

    def weighted(h, l_sum):
        vcols = slice((h // per_vreg) * LANES, (h // per_vreg + 1) * LANES)
        acc = None
        off = 0
        for _, v_ref in srcs:
            n = v_ref.shape[0]
            part = jnp.dot(p_scr[h % 2, :, off:off + n], v_ref[:, vcols], preferred_element_type=F32)
            acc = part if acc is None else acc + part
            off += n
        return acc * (1.0 / l_sum)

    outs = []
    m_next = scores(0)
    for h in range(MLA_HEADS):
        m = m_next
        if h + 1 < MLA_HEADS:
            m_next = scores(h + 1)
        outs.append(weighted(h, probs(h, m)))
    tiles = []
    for t in range(MLA_HEADS // per_vreg):
        tile = outs[t * per_vreg]
        for j in range(1, per_vreg):
            tile = jnp.where(lane >= j * MLA_V, outs[t * per_vreg + j], tile)
        tiles.append(tile)
    o_ref[...] = jnp.concatenate(tiles, axis=1).astype(o_ref.dtype)


def _attn_call(q, kt, v, *, batch, seq, ctx, latent):
    hw = MLA_HEADS * LANES
    cb0 = batch * seq // ctx
    kv_specs = [pl.BlockSpec((hw, ctx), lambda b, i: (0, cb0 + b)),
                pl.BlockSpec((ctx, MLA_WIDTH), lambda b, i: (cb0 + b, 0))]
    args = (q, kt, v)
    if latent:
        per_b = seq // ATT_TQ
        grid = (batch, per_b)
        q_spec = pl.BlockSpec((ATT_TQ, hw), lambda b, i: (b * per_b + i, 0))
        kv_specs += [pl.BlockSpec((hw, seq), lambda b, i: (0, b)),
                     pl.BlockSpec((seq, MLA_WIDTH), lambda b, i: (b, 0))]
        args += (kt, v)
        out_spec = pl.BlockSpec((ATT_TQ, MLA_WIDTH), lambda b, i: (b * per_b + i, 0))
        rows, tq, keys = batch * seq, ATT_TQ, ctx + seq
    else:
        grid = (batch, 1)
        q_spec = pl.BlockSpec((ctx, hw), lambda b, i: (cb0 + b, 0))
        out_spec = pl.BlockSpec((ctx, MLA_WIDTH), lambda b, i: (b, 0))
        rows, tq, keys = batch * ctx, ctx, ctx
    return pl.pallas_call(
        functools.partial(_attn_kernel, n_src=len(kv_specs) // 2),
        out_shape=jax.ShapeDtypeStruct((rows, MLA_WIDTH), BF16),
        grid=grid,
        in_specs=[q_spec] + kv_specs,
        out_specs=out_spec,
        scratch_shapes=[pltpu.VMEM((2, tq, keys), F32), pltpu.VMEM((2, tq, keys), BF16)],
        compiler_params=_params("arbitrary", "arbitrary"),
        name="mla_lat" if latent else "mla_ctx",
    )(*args)


def _outmlp_kernel(*refs, n_x, with_ctx, n_lat_blocks):
    mod_ref = refs[n_x]
    n_y = 6 if with_ctx else 3
    y_refs = refs[n_x + 1:n_x + 1 + n_y]
    gpost_ref, gpre_ref, gmlp_ref, wo_ref, w1_ref, w2_ref, o_ref = refs[n_x + 1 + n_y:]
    if with_ctx:
        is_lat = pl.program_id(0) < n_lat_blocks
        parts = [jnp.where(is_lat, y_refs[2 * i][...], y_refs[2 * i + 1][...]) for i in range(3)]
    else:
        parts = [r[...] for r in y_refs]
    y = jnp.concatenate(parts, axis=1)
    t = jnp.dot(y, wo_ref[...], preferred_element_type=F32)
    x1 = _read_token_rows(refs[:n_x], n_lat_blocks) + mod_ref[2:3, :] * _rms(t, gpost_ref[...])
    h2 = (_rms(x1, gpre_ref[...]) * (1.0 + mod_ref[4:5, :]) + mod_ref[3:4, :]).astype(BF16)
    acc = jnp.zeros(x1.shape, F32)
    for f in range(D_FF // FF_CHUNK):
        cols = slice(f * FF_CHUNK, (f + 1) * FF_CHUNK)
        a = jnp.dot(h2, w1_ref[:, cols], preferred_element_type=F32)
        a = jnp.square(jnp.maximum(a, 0.0)).astype(BF16)
        acc = acc + jnp.dot(a, w2_ref[cols, :], preferred_element_type=F32)
    o_ref[...] = x1 + mod_ref[5:6, :] * _rms(acc, gmlp_ref[...])


def _outmlp_call(xs, mods_l, ys, gpost, gpre, gmlp, wo, w1, w2, *, batch, seq, with_ctx):
    n = sum(a.shape[0] for a in xs) if isinstance(xs, tuple) else xs.shape[0]
    nlb = batch * seq // TM
    nblk = n // TM if with_ctx else nlb
    per_b = seq // TM
    x_specs, x_args = _token_rows(xs, nlb, D_MODEL)
    modrow = lambda i: jnp.where(i < nlb, i // per_b, batch)
    rowblk = lambda w: pl.BlockSpec((TM, w), lambda i: (i, 0))
    resident = lambda a: pl.BlockSpec(a.shape, lambda i: (0,) * a.ndim, pipeline_mode=pl.Buffered(1))
    y_specs, y_args = [], []
    for y_lat, y_ctx in ys:
        w = y_lat.shape[1]
        y_specs.append(pl.BlockSpec((TM, w), lambda i: (jnp.minimum(i, nlb - 1), 0)))
        y_args.append(y_lat)
        if with_ctx:
            y_specs.append(pl.BlockSpec((TM, w), lambda i: (jnp.maximum(i - nlb, 0), 0)))
            y_args.append(y_ctx)
    return pl.pallas_call(
        functools.partial(_outmlp_kernel, n_x=len(x_args), with_ctx=with_ctx, n_lat_blocks=nlb),
        out_shape=jax.ShapeDtypeStruct((nblk * TM, D_MODEL), F32),
        grid=(nblk,),
        in_specs=x_specs + [pl.BlockSpec((None, 6, D_MODEL), lambda i: (modrow(i), 0, 0))] + y_specs
                 + [resident(gpost), resident(gpre), resident(gmlp), resident(wo), resident(w1), resident(w2)],
        out_specs=rowblk(D_MODEL),
        compiler_params=_params("arbitrary"),
        name="outproj_mlp",
    )(*x_args, mods_l, *y_args, gpost, gpre, gmlp, wo, w1, w2)


def _rope_rot_cols(w):
    w4 = w.reshape(w.shape[:-1] + (2, 2, MLA_ROPE // 4))
    return jnp.stack([-w4[..., 1, :], w4[..., 0, :]], axis=-2).reshape(w.shape)


def _zeros_like_cols(w, n):
    return jnp.zeros(w.shape[:-1] + (n,), w.dtype)


def _layout_w_in(w_in):
    o = np.concatenate([[0], np.cumsum(IN_SIZES)])
    qkv, z, beta, dec, lx, lg, cq, ckv, kr = [w_in[..., o[i]:o[i + 1]] for i in range(len(IN_SIZES))]
    zc = functools.partial(_zeros_like_cols, w_in)
    main = jnp.concatenate([qkv, z, lx, lg], axis=-1)
    pad = LANES - MLA_NOPE - MLA_ROPE
    rest = jnp.concatenate([cq, ckv, beta, dec, zc(LANES - 4 * GDN_HEADS),
                            zc(MLA_NOPE), kr, zc(pad), zc(MLA_NOPE), _rope_rot_cols(kr), zc(pad)], axis=-1)
    return main.astype(BF16), rest.astype(BF16)


def _layout_mla(w_uq, w_ukv):
    lead = w_uq.shape[:-1]
    q4 = w_uq.reshape(lead + (MLA_HEADS, MLA_NOPE + MLA_ROPE))
    nope, rope = q4[..., :MLA_NOPE], q4[..., MLA_NOPE:]
    pad = LANES - MLA_NOPE - MLA_ROPE
    wa = jnp.concatenate([nope, rope, _zeros_like_cols(rope, pad)], axis=-1)
    wb = jnp.concatenate([jnp.zeros_like(nope), _rope_rot_cols(rope), _zeros_like_cols(rope, pad)], axis=-1)
    wq = jnp.concatenate([wa.reshape(lead + (-1,)), wb.reshape(lead + (-1,))], axis=-1)
    lead = w_ukv.shape[:-1]
    kv4 = w_ukv.reshape(lead + (MLA_HEADS, MLA_NOPE + MLA_V))
    k_nope, v = kv4[..., :MLA_NOPE], kv4[..., MLA_NOPE:]
    wkn = jnp.concatenate([k_nope, _zeros_like_cols(k_nope, LANES - MLA_NOPE)], axis=-1).reshape(lead + (-1,))
    return wq.astype(BF16), wkn.astype(BF16), v.reshape(lead + (-1,)).astype(BF16)


def _rope_table(seq):
    rows = seq // GRID_W
    row = jnp.repeat(jnp.arange(rows, dtype=F32), GRID_W)
    col = jnp.tile(jnp.arange(GRID_W, dtype=F32), rows)
    half = MLA_ROPE // 2
    inv = ROPE_BASE ** (-jnp.arange(0, half, 2, dtype=F32) / half)
    ang = jnp.stack([row[:, None] * inv, col[:, None] * inv], axis=1)
    ang = jnp.concatenate([ang, ang], axis=-1).reshape(seq, MLA_ROPE)
    cos = jnp.concatenate([jnp.cos(ang), jnp.ones((TM, MLA_ROPE), F32)], axis=0)
    sin = jnp.concatenate([jnp.sin(ang), jnp.zeros((TM, MLA_ROPE), F32)], axis=0)
    n = seq + TM
    one, zero = jnp.ones((n, MLA_NOPE), F32), jnp.zeros((n, MLA_NOPE), F32)
    pad = jnp.zeros((n, LANES - MLA_NOPE - MLA_ROPE), F32)
    scale = (MLA_NOPE + MLA_ROPE) ** -0.5 * math.log2(math.e)
    return jnp.concatenate([scale * one, scale * cos, pad, zero, scale * sin, pad,
                            zero, cos, pad, zero, sin, pad], axis=1)


def _layout_lru(w_a, b_a, w_i, b_i):
    def blockdiag(w):
        g, n = w.shape[1], w.shape[2]
        out = jnp.zeros((w.shape[0], g * n, g * n), w.dtype)
        for j in range(g):
            out = out.at[:, j * n:(j + 1) * n, j * n:(j + 1) * n].set(w[:, j])
        return out

    da, di = blockdiag(w_a), blockdiag(w_i)
    w = jnp.concatenate([da[0], di[0], da[1], di[1]], axis=-1).astype(BF16)
    b = jnp.concatenate([b_a[0], b_i[0], b_a[1], b_i[1]], axis=-1)[None, :]
    return w, b


def kernel(x, c, ctx, c_ctx, w_ada, b_ada, g_attn_pre, g_attn_post, g_mlp_pre, g_mlp_post, w_in, gdn_conv_w,
           gdn_a_log, gdn_dt_bias, gdn_norm_w, lru_conv_w, lru_conv_b, lru_w_a, lru_b_a, lru_w_i, lru_b_i,
           lru_lambda, mla_q_norm, mla_w_uq, mla_kv_norm, mla_w_ukv, w_out, w_mlp1, w_mlp2):
    batch, seq, _ = x.shape
    n_ctx = ctx.shape[1]
    depth = w_ada.shape[0]
    assert seq % TM == 0 and (batch * n_ctx) % TM == 0 and n_ctx % CONV_ROWS == 0 and seq % GRID_W == 0
    assert (batch * seq) % n_ctx == 0 and batch + 1 <= 2 * SUBLANES
    dims = dict(batch=batch, seq=seq, ctx=n_ctx)

    cond = jnp.concatenate([c, c_ctx[None, :], jnp.zeros((2 * SUBLANES - batch - 1, D_MODEL), F32)], axis=0)
    mods = _ada_call(cond, w_ada, b_ada).reshape(depth, 2 * SUBLANES, 6, D_MODEL)
    tab = _rope_table(seq)
    xs = (x.reshape(batch * seq, D_MODEL), ctx.reshape(batch * n_ctx, D_MODEL))
    row = lambda a: a[None, :]

    for l in range(depth):
        last = l == depth - 1
        wm, wr = _layout_w_in(w_in[l])
        wq, wkn, wv = _layout_mla(mla_w_uq[l], mla_w_ukv[l])
        pad = jnp.zeros((LANES - 4 * GDN_HEADS,), F32)
        ald = jnp.stack([jnp.concatenate([jnp.zeros((2 * GDN_HEADS,), F32), gdn_a_log[l].reshape(-1), pad]),
                         jnp.concatenate([jnp.zeros((2 * GDN_HEADS,), F32), gdn_dt_bias[l].reshape(-1), pad])])
        p, gg, q, kt, v = _inproj_call(xs, mods[l], row(g_attn_pre[l]), wm, wr, row(mla_q_norm[l]), wq,
                                      row(mla_kv_norm[l]), wkn, wv, tab, ald, batch=batch, seq=seq)
        y_gdn = _gdn_call(p, gg, gdn_conv_w[l], row(gdn_norm_w[l]), **dims)
        w_blk, b_blk = _layout_lru(lru_w_a[l], lru_b_a[l], lru_w_i[l], lru_b_i[l])
        y_lru = _lru_call(p, lru_conv_w[l], row(lru_conv_b[l]), w_blk, b_blk, lru_lambda[l], **dims)
        y_att = (_attn_call(q, kt, v, latent=True, **dims),
                 None if last else _attn_call(q, kt, v, latent=False, **dims))
        xs = _outmlp_call(xs, mods[l], (y_gdn, y_lru, y_att), row(g_attn_post[l]), row(g_mlp_pre[l]),
                          row(g_mlp_post[l]), w_out[l].astype(BF16), w_mlp1[l].astype(BF16),
                          w_mlp2[l].astype(BF16), batch=batch, seq=seq, with_ctx=not last)
    return xs.reshape(batch, seq, D_MODEL)
```

```python
import functools
import math

import numpy as np
import jax
import jax.numpy as jnp
from jax import lax
from jax.experimental import pallas as pl
from jax.experimental.pallas import tpu as pltpu

F32 = jnp.float32
BF16 = jnp.bfloat16

D_MODEL = 1024
EPS = 1e-6
GRID_W = 64
CONV_WIDTH = 4
GDN_HEAD_DIM = 128
GDN_HEADS = 4
GDN_WIDTH = GDN_HEADS * GDN_HEAD_DIM
GDN_CHUNK = 64
LRU_WIDTH = 256
LRU_BLOCKS = 4
LRU_C = 8.0
MLA_V = 64
MLA_HEADS = 4
MLA_WIDTH = MLA_HEADS * MLA_V
MLA_NOPE = 64
MLA_ROPE = 32
MLA_Q_RANK = 256
MLA_KV_RANK = 128
ROPE_BASE = 10000.0
D_FF = 4 * D_MODEL
IN_SIZES = (3 * GDN_WIDTH, GDN_WIDTH, 2 * GDN_HEADS, 2 * GDN_HEADS, LRU_WIDTH, LRU_WIDTH,
            MLA_Q_RANK, MLA_KV_RANK, MLA_ROPE)

LANES = 128
SUBLANES = 8
TM = 512
MAIN_COLS = 3 * GDN_WIDTH + GDN_WIDTH + 2 * LRU_WIDTH
REST_COLS = MLA_Q_RANK + MLA_KV_RANK + 3 * LANES
CONV_ROWS = 256
ATT_TQ = 512
ATT_SUB = 256
ATT_KB = 512
FF_CHUNK = 1024
VMEM_LIMIT = 56 * 1024 * 1024

_NT = (((1,), (1,)), ((), ()))
_TN = (((0,), (0,)), ((), ()))


def _rms(x, g):
    return x * lax.rsqrt(jnp.mean(x * x, axis=-1, keepdims=True) + EPS) * g


def _bdot(a, b):
    return jnp.dot(a.astype(BF16), b.astype(BF16), preferred_element_type=F32)


def _token_rows(arr, n_lat_blocks, width):
    if isinstance(arr, tuple):
        return ([pl.BlockSpec((TM, width), lambda i: (jnp.minimum(i, n_lat_blocks - 1), 0)),
                 pl.BlockSpec((TM, width), lambda i: (jnp.maximum(i - n_lat_blocks, 0), 0))], list(arr))
    return [pl.BlockSpec((TM, width), lambda i: (i, 0))], [arr]


def _read_token_rows(refs, n_lat_blocks):
    if len(refs) == 2:
        return jnp.where(pl.program_id(0) < n_lat_blocks, refs[0][...], refs[1][...])
    return refs[0][...]


def _params(*sem):
    return pltpu.CompilerParams(dimension_semantics=sem, vmem_limit_bytes=VMEM_LIMIT)


def _ada_kernel(c_ref, w_ref, b_ref, o_ref):
    o_ref[...] = _bdot(jax.nn.silu(c_ref[...]), w_ref[...]) + b_ref[...]


def _ada_call(cond, w_ada, b_ada):
    depth = w_ada.shape[0]
    rows = cond.shape[0]
    return pl.pallas_call(
        _ada_kernel,
        out_shape=jax.ShapeDtypeStruct((depth, rows, 6 * D_MODEL), F32),
        grid=(depth, 6),
        in_specs=[pl.BlockSpec((rows, D_MODEL), lambda l, j: (0, 0)),
                  pl.BlockSpec((None, D_MODEL, D_MODEL), lambda l, j: (l, 0, j)),
                  pl.BlockSpec((None, 1, D_MODEL), lambda l, j: (l, 0, j))],
        out_specs=pl.BlockSpec((None, rows, D_MODEL), lambda l, j: (l, 0, j)),
        compiler_params=_params("arbitrary", "arbitrary"),
        name="adaln",
    )(cond, w_ada, b_ada.reshape(depth, 1, 6 * D_MODEL))


def _gdn_gates(raw, a_log, dt_b):
    rows = raw.shape[0]
    lane = lax.broadcasted_iota(jnp.int32, raw.shape, 1)
    pos = lax.broadcasted_iota(jnp.int32, raw.shape, 0) % GDN_CHUNK
    g = -jnp.exp(a_log) * jax.nn.softplus(raw + dt_b)
    cum_f, cum_b = g, g
    s = 1
    while s < GDN_CHUNK:
        cum_f = cum_f + jnp.where(pos >= s, pltpu.roll(cum_f, s, axis=0), 0.0)
        cum_b = cum_b + jnp.where(pos < GDN_CHUNK - s, pltpu.roll(cum_b, rows - s, axis=0), 0.0)
        s *= 2
    return jnp.where(lane < 2 * GDN_HEADS, jax.nn.sigmoid(raw), jnp.where(lane < 3 * GDN_HEADS, cum_f, cum_b))


def _inproj_kernel(*refs, n_x, n_lat_blocks):
    (mod_ref, g_ref, wm_ref, wr_ref, qn_ref, wq_ref, kvn_ref, wkn_ref, wv_ref, tab_ref, ald_ref,
     p_ref, gg_ref, q_ref, kt_ref, v_ref) = refs[n_x:]
    h = _rms(_read_token_rows(refs[:n_x], n_lat_blocks), g_ref[...]) * (1.0 + mod_ref[1:2, :]) + mod_ref[0:1, :]
    hb = h.astype(BF16)
    p_ref[...] = jnp.dot(hb, wm_ref[...], preferred_element_type=F32)
    r = jnp.dot(hb, wr_ref[...], preferred_element_type=F32)
    o0 = MLA_Q_RANK
    o1 = o0 + MLA_KV_RANK
    cq = r[:, :o0]
    ckv = r[:, o0:o1]
    gg_ref[...] = _gdn_gates(r[:, o1:o1 + LANES], ald_ref[0:1, :], ald_ref[1:2, :])
    kr_a = r[:, o1 + LANES:o1 + 2 * LANES]
    kr_b = r[:, o1 + 2 * LANES:o1 + 3 * LANES]
    tab = tab_ref[...]
    tile = lambda t: jnp.concatenate([t] * MLA_HEADS, axis=1)
    q2 = _bdot(_rms(cq, qn_ref[...]), wq_ref[...])
    hw = MLA_HEADS * LANES
    q_ref[...] = (q2[:, :hw] * tile(tab[:, :LANES]) + q2[:, hw:] * tile(tab[:, LANES:2 * LANES])).astype(BF16)
    ckvn = _rms(ckv, kvn_ref[...]).astype(BF16)
    kr = kr_a * tab[:, 2 * LANES:3 * LANES] + kr_b * tab[:, 3 * LANES:]
    kt_ref[...] = (jnp.dot(ckvn, wkn_ref[...], preferred_element_type=F32) + tile(kr)).T.astype(BF16)
    v_ref[...] = jnp.dot(ckvn, wv_ref[...], preferred_element_type=F32).astype(BF16)


def _inproj_call(xs, mods_l, g, wm, wr, qn, wq, kvn, wkn, wv, tab, ald, *, batch, seq):
    n = sum(a.shape[0] for a in xs) if isinstance(xs, tuple) else xs.shape[0]
    nblk = n // TM
    nlb = batch * seq // TM
    per_b = seq // TM
    x_specs, x_args = _token_rows(xs, nlb, D_MODEL)
    modrow = lambda i: jnp.where(i < nlb, i // per_b, batch)
    tabrow = lambda i: jnp.where(i < nlb, i % per_b, per_b)
    full = lambda a: pl.BlockSpec(a.shape, lambda i: (0,) * a.ndim)
    rowblk = lambda w: pl.BlockSpec((TM, w), lambda i: (i, 0))
    hw = MLA_HEADS * LANES
    return pl.pallas_call(
        functools.partial(_inproj_kernel, n_x=len(x_args), n_lat_blocks=nlb),
        out_shape=(jax.ShapeDtypeStruct((n, MAIN_COLS), F32),
                   jax.ShapeDtypeStruct((n, LANES), F32),
                   jax.ShapeDtypeStruct((n, hw), BF16),
                   jax.ShapeDtypeStruct((hw, n), BF16),
                   jax.ShapeDtypeStruct((n, MLA_WIDTH), BF16)),
        grid=(nblk,),
        in_specs=x_specs + [
                  pl.BlockSpec((None, 6, D_MODEL), lambda i: (modrow(i), 0, 0)),
                  full(g), full(wm), full(wr), full(qn), full(wq), full(kvn), full(wkn), full(wv),
                  pl.BlockSpec((TM, 4 * LANES), lambda i: (tabrow(i), 0)), full(ald)],
        out_specs=(rowblk(MAIN_COLS), rowblk(LANES), rowblk(hw), pl.BlockSpec((hw, TM), lambda i: (0, i)),
                   rowblk(MLA_WIDTH)),
        compiler_params=_params("arbitrary"),
        name="inproj",
    )(*x_args, mods_l, g, wm, wr, qn, wq, kvn, wkn, wv, tab, ald)


def _conv_chunk(src_ref, r, n_chunks, n_rows, w, rows=CONV_ROWS):
    start = pl.multiple_of(r * rows, rows)
    cur = src_ref[pl.ds(start, rows), :]
    prev_start = pl.multiple_of(jnp.maximum(start - SUBLANES, 0), SUBLANES)
    prev = jnp.where(r > 0, src_ref[pl.ds(prev_start, SUBLANES), :], 0.0)
    next_start = pl.multiple_of(jnp.minimum(start + rows, n_rows - SUBLANES), SUBLANES)
    nxt = jnp.where(r < n_chunks - 1, src_ref[pl.ds(next_start, SUBLANES), :], 0.0)
    cat = jnp.concatenate([prev, cur, nxt], axis=0)
    tot = rows + 2 * SUBLANES
    sl = slice(SUBLANES, SUBLANES + rows)
    acc = w[2:3, :] * cur
    acc = acc + w[0:1, :] * pltpu.roll(cat, 2, axis=0)[sl]
    acc = acc + w[1:2, :] * pltpu.roll(cat, 1, axis=0)[sl]
    acc = acc + w[3:4, :] * pltpu.roll(cat, tot - 1, axis=0)[sl]
    return start, acc


GDN_UNROLL = 8


def _gdn_kernel(ql_ref, kl_ref, vl_ref, qc_ref, kc_ref, vc_ref, zl_ref, zc_ref, ggl_ref, ggc_ref,
                cwq_ref, cwk_ref, cwv_ref, nw_ref, yl_ref, yc_ref,
                qs, ks, vs, gs, o_loc, o_seq, lhs_r, upd_r, gl_r, *, seq, ctx):
    head = pl.program_id(1)
    C = GDN_CHUNK
    C2 = 2 * C
    hd = GDN_HEAD_DIM
    U = GDN_UNROLL
    n_ctx = ctx // C
    n_tot = (ctx + seq) // C
    pad = U - n_ctx
    n_groups = (n_tot - n_ctx) // U

    cws = (cwq_ref[...], cwk_ref[...], cwv_ref[...])

    def conv_block(srcs, j, n_rows, off):
        for src, dst, w, kind in zip(srcs, (qs, ks, vs), cws, "qkv"):
            start, acc = _conv_chunk(src, j, n_rows // CONV_ROWS, n_rows, w)
            y = jax.nn.silu(acc)
            if kind != "v":
                y = y * lax.rsqrt(jnp.sum(y * y, axis=-1, keepdims=True) + EPS)
            if kind == "q":
                y = y * (hd ** -0.5)
            dst[pl.ds(off + start, CONV_ROWS), :] = y

    def prep_gates(src_ref, off, n_rows):
        def body(r, carry):
            start = pl.multiple_of(r * CONV_ROWS, CONV_ROWS)
            gs[pl.ds(off + start, CONV_ROWS), :] = src_ref[pl.ds(start, CONV_ROWS), :]
            return carry

        lax.fori_loop(0, n_rows // CONV_ROWS, body, 0)

    prep_gates(ggc_ref, 0, ctx)
    prep_gates(ggl_ref, ctx, seq)
    lat_srcs = (ql_ref, kl_ref, vl_ref)
    n_blk = seq // CONV_ROWS
    per_side = U * C // CONV_ROWS
    for j in range(ctx // CONV_ROWS):
        conv_block((qc_ref, kc_ref, vc_ref), j, ctx, 0)
    for j in sorted(set(range(min(per_side, n_blk))) | set(range(max(n_blk - per_side, 0), n_blk))):
        conv_block(lat_srcs, j, seq, ctx)
    conv_groups = max(n_blk // (2 * per_side) - 1, 0)

    lane = lax.broadcasted_iota(jnp.int32, (C, LANES), 1)
    sub_t = lax.broadcasted_iota(jnp.int32, (LANES, C), 0)
    bwd_lane = (lane % (2 * GDN_HEADS)) >= GDN_HEADS
    prow = lax.broadcasted_iota(jnp.int32, (C2, C2), 0)
    pcol = lax.broadcasted_iota(jnp.int32, (C2, C2), 1)
    sgn = 1 - 2 * (prow // C)
    incl_p = ((prow // C) == (pcol // C)) & ((prow - pcol) * sgn >= 0)
    eye_p = (prow == pcol).astype(F32)
    off_diag = []
    g = 1
    while g < C:
        half = (prow % (2 * g)) // g - (pcol % (2 * g)) // g
        off_diag.append(((prow // (2 * g)) == (pcol // (2 * g))) & (half * sgn == 1))
        g *= 2

    def step_chunks(s):
        return s, jnp.where(s < n_ctx, n_ctx - 1 - s, n_tot - 1 - (s - n_ctx))

    def chunk_rows(c):
        return pl.ds(pl.multiple_of(c * C, C), C)

    def seq_step(entry, v, carry):
        cf, cb = step_chunks(v - pad)
        chunks = (jnp.where(v < pad, n_tot, cf), jnp.where(v < pad, n_tot, cb))
        new = []
        for d in range(2):
            res = jnp.dot(lhs_r[entry, d], carry[d].astype(BF16), preferred_element_type=F32)
            o_seq[d, chunk_rows(chunks[d]), :] = res[:C]
            new.append(carry[d] * gl_r[entry, d, 0:1, :] + upd_r[entry, d] - res[C:])
        return tuple(new)

    def local_steps(steps, entries, tick, after_loads=lambda: None):
        U_ = range(len(steps))
        q2, k2, v2, g_col, b_col, g_last, decay, rows = [], [], [], [], [], [], [], []
        for s in steps:
            cf, cb = step_chunks(s)
            rf, rb = chunk_rows(cf), chunk_rows(cb)
            rows.append((rf, rb))
            q2.append(jnp.concatenate([qs[rf, :], qs[rb, :]], axis=0))
            k2.append(jnp.concatenate([ks[rf, :], ks[rb, :]], axis=0))
            v2.append(jnp.concatenate([vs[rf, :], vs[rb, :]], axis=0))
            gg = jnp.where(bwd_lane, gs[rb, :], gs[rf, :])
            gg_t = gg.T
            g_cols, b_cols, g_rows = [], [], []
            for d in range(2):
                g_lane = 2 * GDN_HEADS + d * GDN_HEADS + head
                g_cols.append(jnp.sum(jnp.where(lane == g_lane, gg, 0.0), axis=-1, keepdims=True))
                b_cols.append(jnp.sum(jnp.where(lane == d * GDN_HEADS + head, gg, 0.0), axis=-1, keepdims=True))
                g_rows.append(jnp.sum(jnp.where(sub_t == g_lane, gg_t, 0.0), axis=0, keepdims=True))
            g_last.append((g_cols[0][C - 1:C, :], g_cols[1][0:1, :]))
            g_col.append(jnp.concatenate(g_cols, axis=0))
            b_col.append(jnp.concatenate(b_cols, axis=0))
            g_row = jnp.concatenate(g_rows, axis=1)
            decay.append(jnp.exp(jnp.where(incl_p, g_col[-1] - g_row, -jnp.inf)))
        after_loads()
        tick()
        r_qk = [lax.dot_general(jnp.concatenate([q2[u], k2[u]], axis=0).astype(BF16), k2[u].astype(BF16), _NT,
                                preferred_element_type=F32) for u in U_]
        attn = [r_qk[u][:C2] * decay[u] for u in U_]
        low = [r_qk[u][C2:] * b_col[u] * decay[u] for u in U_]
        tick()
        t_inv = [eye_p - jnp.where(off_diag[0], low[u], 0.0) for u in U_]
        for m in off_diag[1:]:
            xt = [_bdot(jnp.where(m, low[u], 0.0), t_inv[u]) for u in U_]
            t_inv = [t_inv[u] - _bdot(t_inv[u], xt[u]) for u in U_]
            tick()
        e_g = [jnp.exp(g_col[u]) for u in U_]
        x = [_bdot(t_inv[u], jnp.concatenate([k2[u] * (e_g[u] * b_col[u]), v2[u] * b_col[u]], axis=1))
             for u in U_]
        tick()
        ax = [_bdot(attn[u], x[u]) for u in U_]
        tick()
        zero = jnp.zeros((C, hd), F32)
        kx = []
        for u in U_:
            gl_col = jnp.concatenate([jnp.broadcast_to(g_last[u][0], (C, 1)),
                                      jnp.broadcast_to(g_last[u][1], (C, 1))], axis=0)
            kg = k2[u] * jnp.exp(gl_col - g_col[u])
            kg_blk = jnp.concatenate([jnp.concatenate([kg[:C], zero], axis=1),
                                      jnp.concatenate([zero, kg[C:]], axis=1)], axis=0)
            kx.append(lax.dot_general(kg_blk.astype(BF16), x[u].astype(BF16), _TN,
                                      preferred_element_type=F32))
        tick()
        for u in U_:
            q_eff = q2[u] * e_g[u] - ax[u][:, :hd]
            for d in range(2):
                lhs_r[entries[u], d, 0:C, :] = q_eff[d * C:(d + 1) * C].astype(BF16)
                lhs_r[entries[u], d, C:C + hd, :] = kx[u][d * hd:(d + 1) * hd, :hd].astype(BF16)
                upd_r[entries[u], d] = kx[u][d * hd:(d + 1) * hd, hd:]
                gl_r[entries[u], d] = jnp.broadcast_to(jnp.exp(g_last[u][d]), (SUBLANES, LANES))
                o_loc[d, rows[u][d], :] = ax[u][d * C:(d + 1) * C, hd:]

    for e in range(pad):
        for d in range(2):
            lhs_r[e, d] = jnp.zeros((C + hd, LANES), BF16)
            upd_r[e, d] = jnp.zeros((hd, LANES), F32)
            gl_r[e, d] = jnp.ones((SUBLANES, LANES), F32)
    local_steps(list(range(n_ctx)), [pad + u for u in range(n_ctx)], lambda: None)

    def group(i, carry, with_conv):
        state = {"carry": carry, "next": 0}

        def conv_next():
            for t in range(per_side):
                conv_block(lat_srcs, i * per_side + t, seq, ctx)
                conv_block(lat_srcs, n_blk - 1 - i * per_side - t, seq, ctx)

        prev_base = ((i - 1) % 2) * U

        def tick():
            if state["next"] < U:
                u = state["next"]
                state["carry"] = seq_step(prev_base + u, (i - 1) * U + u, state["carry"])
                state["next"] = u + 1

        base = (i % 2) * U
        local_steps([i * U + u - pad for u in range(U)], [base + u for u in range(U)], tick,
                    conv_next if with_conv else (lambda: None))
        while state["next"] < U:
            tick()
        return state["carry"]

    zero = jnp.zeros((hd, hd), F32)
    carry = lax.fori_loop(1, 1 + conv_groups, functools.partial(group, with_conv=True), (zero, zero))
    carry = lax.fori_loop(1 + conv_groups, n_groups + 1, functools.partial(group, with_conv=False), carry)
    def finish(z_ref, y_ref, off, j):
        start = j * CONV_ROWS
        rows = pl.ds(off + start, CONV_ROWS)
        o = (o_loc[0, rows, :] + o_loc[1, rows, :]) + (o_seq[0, rows, :] + o_seq[1, rows, :])
        y = _rms(o, nw_ref[...]) * jax.nn.silu(z_ref[pl.ds(start, CONV_ROWS), :])
        y_ref[pl.ds(start, CONV_ROWS), :] = y.astype(y_ref.dtype)

    touched = set(range(min(per_side, n_blk))) | set(range(max(n_blk - per_side, 0), n_blk))
    segs = {"ctx": (zc_ref, yc_ref, 0), "lat": (zl_ref, yl_ref, ctx)}
    every = [("ctx", j) for j in range(ctx // CONV_ROWS)] + [("lat", j) for j in range(n_blk)]
    early = [key for key in every if n_groups > 0 and (key[0] == "ctx" or key[1] not in touched)]
    for u in range(U):
        carry = seq_step((n_groups % 2) * U + u, n_groups * U + u, carry)
        for seg, j in early[u::U]:
            finish(*segs[seg], j)
    for seg, j in every:
        if (seg, j) not in early:
            finish(*segs[seg], j)


def _gdn_call(p, gg, conv_w, norm_w, *, batch, seq, ctx):
    H = GDN_HEADS
    C = GDN_CHUNK
    cb0 = batch * seq // ctx
    lat = lambda cblk: pl.BlockSpec((seq, LANES), lambda b, h: (b, cblk * H + h))
    cx = lambda cblk: pl.BlockSpec((ctx, LANES), lambda b, h: (cb0 + b, cblk * H + h))
    cw = lambda cblk: pl.BlockSpec((CONV_WIDTH, LANES), lambda b, h: (0, cblk * H + h))
    T = seq + ctx
    assert ctx // C <= GDN_UNROLL and (seq // C) % GDN_UNROLL == 0
    return pl.pallas_call(
        functools.partial(_gdn_kernel, seq=seq, ctx=ctx),
        out_shape=(jax.ShapeDtypeStruct((batch * seq, GDN_WIDTH), BF16),
                   jax.ShapeDtypeStruct((batch * ctx, GDN_WIDTH), BF16)),
        grid=(batch, H),
        in_specs=[lat(0), lat(1), lat(2), cx(0), cx(1), cx(2), lat(3), cx(3),
                  pl.BlockSpec((seq, LANES), lambda b, h: (b, 0)),
                  pl.BlockSpec((ctx, LANES), lambda b, h: (cb0 + b, 0)),
                  cw(0), cw(1), cw(2),
                  pl.BlockSpec((1, LANES), lambda b, h: (0, 0))],
        out_specs=(pl.BlockSpec((seq, LANES), lambda b, h: (b, h)),
                   pl.BlockSpec((ctx, LANES), lambda b, h: (b, h))),
        scratch_shapes=[pltpu.VMEM((T, LANES), F32)] * 4
                       + [pltpu.VMEM((2, T, LANES), F32), pltpu.VMEM((2, T + C, LANES), F32),
                          pltpu.VMEM((2 * GDN_UNROLL, 2, C + GDN_HEAD_DIM, LANES), BF16),
                          pltpu.VMEM((2 * GDN_UNROLL, 2, GDN_HEAD_DIM, LANES), F32),
                          pltpu.VMEM((2 * GDN_UNROLL, 2, SUBLANES, LANES), F32)],
        compiler_params=_params("arbitrary", "arbitrary"),
        name="gdn",
    )(p, p, p, p, p, p, p, p, gg, gg, conv_w, conv_w, conv_w, norm_w)


def _lru_scan(a, u, fwd):
    n = a.shape[0]
    row = lax.broadcasted_iota(jnp.int32, a.shape, 0)
    s = 1
    while s < n:
        shift = s if fwd else n - s
        valid = (row >= s) if fwd else (row < n - s)
        u = jnp.where(valid, a * pltpu.roll(u, shift, axis=0) + u, u)
        a = jnp.where(valid, a * pltpu.roll(a, shift, axis=0), a)
        s *= 2
    return a, u


def _lru_kernel(xl_ref, xc_ref, gl_ref, gc_ref, cw_ref, cb_ref, w_ref, b_ref, lam_ref, yl_ref, yc_ref,
                ab_s, ub_s, hf_s, *, seq, ctx):
    W = LRU_WIDTH
    neg_log_base = jax.nn.softplus(-lam_ref[...])
    cw = cw_ref[...]

    def forward(src_ref, off, n_rows, h):
        n_chunks = n_rows // CONV_ROWS

        def body(r, h):
            start, acc = _conv_chunk(src_ref, r, n_chunks, n_rows, cw)
            xc = acc + cb_ref[...]
            gates = jax.nn.sigmoid(_bdot(xc, w_ref[...]) + b_ref[...])
            coef = []
            for d in range(2):
                rg = gates[:, 2 * d * W:(2 * d + 1) * W]
                ig = gates[:, (2 * d + 1) * W:(2 * d + 2) * W]
                log_a = -LRU_C * rg * neg_log_base[d:d + 1, :]
                a = jnp.exp(log_a)
                coef.append((a, jnp.sqrt(-jnp.tanh(log_a) * (a * a + 1.0)) * (ig * xc)))
            rows = pl.ds(off + start, CONV_ROWS)
            ab_s[rows, :] = coef[1][0]
            ub_s[rows, :] = coef[1][1]
            a_cum, h0 = _lru_scan(coef[0][0], coef[0][1], True)
            hf = h0 + a_cum * h
            hf_s[rows, :] = hf
            return hf[CONV_ROWS - 1:CONV_ROWS, :]

        return lax.fori_loop(0, n_chunks, body, h)

    def backward(gate_ref, y_ref, off, n_rows, h):
        n_chunks = n_rows // CONV_ROWS

        def body(i, h):
            r = n_chunks - 1 - i
            start = pl.multiple_of(r * CONV_ROWS, CONV_ROWS)
            rows = pl.ds(off + start, CONV_ROWS)
            a_cum, h0 = _lru_scan(ab_s[rows, :], ub_s[rows, :], False)
            hb = h0 + a_cum * h
            y = (hf_s[rows, :] + hb) * jax.nn.gelu(gate_ref[pl.ds(start, CONV_ROWS), :])
            y_ref[pl.ds(start, CONV_ROWS), :] = y.astype(y_ref.dtype)
            return hb[0:1, :]

        return lax.fori_loop(0, n_chunks, body, h)

    zero = jnp.zeros((1, W), F32)
    forward(xl_ref, ctx, seq, forward(xc_ref, 0, ctx, zero))
    backward(gl_ref, yl_ref, ctx, seq, backward(gc_ref, yc_ref, 0, ctx, zero))


def _lru_call(p, conv_w, conv_b, w_blk, b_blk, lam, *, batch, seq, ctx):
    cb0 = batch * seq // ctx
    x_col = (3 * GDN_WIDTH + GDN_WIDTH) // LRU_WIDTH
    T = seq + ctx
    full = lambda a: pl.BlockSpec(a.shape, lambda b: (0,) * a.ndim)
    return pl.pallas_call(
        functools.partial(_lru_kernel, seq=seq, ctx=ctx),
        out_shape=(jax.ShapeDtypeStruct((batch * seq, LRU_WIDTH), BF16),
                   jax.ShapeDtypeStruct((batch * ctx, LRU_WIDTH), BF16)),
        grid=(batch,),
        in_specs=[pl.BlockSpec((seq, LRU_WIDTH), lambda b: (b, x_col)),
                  pl.BlockSpec((ctx, LRU_WIDTH), lambda b: (cb0 + b, x_col)),
                  pl.BlockSpec((seq, LRU_WIDTH), lambda b: (b, x_col + 1)),
                  pl.BlockSpec((ctx, LRU_WIDTH), lambda b: (cb0 + b, x_col + 1)),
                  full(conv_w), full(conv_b), full(w_blk), full(b_blk), full(lam)],
        out_specs=(pl.BlockSpec((seq, LRU_WIDTH), lambda b: (b, 0)),
                   pl.BlockSpec((ctx, LRU_WIDTH), lambda b: (b, 0))),
        scratch_shapes=[pltpu.VMEM((T, LRU_WIDTH), F32)] * 3,
        compiler_params=_params("arbitrary"),
        name="rglru",
    )(p, p, p, p, conv_w, conv_b, w_blk, b_blk, lam)


def _attn_kernel(*refs, n_src):
    q_ref = refs[0]
    srcs = [(refs[1 + 2 * i], refs[2 + 2 * i]) for i in range(n_src)]
    o_ref, s_scr, p_scr = refs[1 + 2 * n_src:]
    tq = s_scr.shape[1]
    lane = lax.broadcasted_iota(jnp.int32, (tq, LANES), 1)
    per_vreg = LANES // MLA_V
    col_blocks = []
    off = 0
    for i, (kt_ref, _) in enumerate(srcs):
        n = kt_ref.shape[1]
        for st in range(0, n, ATT_KB):
            col_blocks.append((i, st, off + st, min(ATT_KB, n - st)))
        off += n

    def fold(acc, x, op):
        for i in range(x.shape[1] // LANES):
            acc = op(acc, x[:, i * LANES:(i + 1) * LANES])
        return acc

    items = [(r, h) for r in range(q_ref.shape[0] // tq) for h in range(MLA_HEADS)]

    def scores(n):
        r, h = items[n]
        hrows = slice(h * LANES, (h + 1) * LANES)
        qh = q_ref[r * tq:(r + 1) * tq, hrows]
        m_part = jnp.full((tq, LANES), -jnp.inf, F32)
        for i, st, dst, w in col_blocks:
            s = jnp.dot(qh, srcs[i][0][hrows, st:st + w], preferred_element_type=F32)
            s_scr[n % 2, :, dst:dst + w] = s
            m_part = fold(m_part, s, jnp.maximum)
        return jnp.max(m_part, axis=-1, keepdims=True)

    def probs(n, m):
        l_part = jnp.zeros((tq, LANES), F32)
        for _, _, dst, w in col_blocks:
            p = jnp.exp2(s_scr[n % 2, :, dst:dst + w] - m)
            p_scr[n % 2, :, dst:dst + w] = p.astype(BF16)
            l_part = fold(l_part, p, jnp.add)
        return jnp.sum(l_part, axis=-1, keepdims=True)

    def weighted(n, l_sum):
        h = items[n][1]
        vcols = slice((h // per_vreg) * LANES, (h // per_vreg + 1) * LANES)
        acc = None
        off = 0
        for _, v_ref in srcs:
            n_keys = v_ref.shape[0]
            part = jnp.dot(p_scr[n % 2, :, off:off + n_keys], v_ref[:, vcols], preferred_element_type=F32)
            acc = part if acc is None else acc + part
            off += n_keys
        return acc * (1.0 / l_sum)

    outs = []
    m_next = scores(0)
    for n in range(len(items)):
        m = m_next
        if n + 1 < len(items):
            m_next = scores(n + 1)
        outs.append(weighted(n, probs(n, m)))
        if items[n][1] == MLA_HEADS - 1:
            r = items[n][0]
            heads = outs[-MLA_HEADS:]
            tiles = []
            for t in range(MLA_HEADS // per_vreg):
                tile = heads[t * per_vreg]
                for j in range(1, per_vreg):
                    tile = jnp.where(lane >= j * MLA_V, heads[t * per_vreg + j], tile)
                tiles.append(tile)
            o_ref[r * tq:(r + 1) * tq, :] = jnp.concatenate(tiles, axis=1).astype(o_ref.dtype)


def _attn_call(q, kt, v, *, batch, seq, ctx, latent):
    hw = MLA_HEADS * LANES
    cb0 = batch * seq // ctx
    kv_specs = [pl.BlockSpec((hw, ctx), lambda b, i: (0, cb0 + b)),
                pl.BlockSpec((ctx, MLA_WIDTH), lambda b, i: (cb0 + b, 0))]
    args = (q, kt, v)
    if latent:
        per_b = seq // ATT_TQ
        grid = (batch, per_b)
        q_spec = pl.BlockSpec((ATT_TQ, hw), lambda b, i: (b * per_b + i, 0))
        kv_specs += [pl.BlockSpec((hw, seq), lambda b, i: (0, b)),
                     pl.BlockSpec((seq, MLA_WIDTH), lambda b, i: (b, 0))]
        args += (kt, v)
        out_spec = pl.BlockSpec((ATT_TQ, MLA_WIDTH), lambda b, i: (b * per_b + i, 0))
        rows, tq, keys = batch * seq, ATT_SUB, ctx + seq
    else:
        grid = (batch, 1)
        q_spec = pl.BlockSpec((ctx, hw), lambda b, i: (cb0 + b, 0))
        out_spec = pl.BlockSpec((ctx, MLA_WIDTH), lambda b, i: (b, 0))
        rows, tq, keys = batch * ctx, ctx, ctx
    return pl.pallas_call(
        functools.partial(_attn_kernel, n_src=len(kv_specs) // 2),
        out_shape=jax.ShapeDtypeStruct((rows, MLA_WIDTH), BF16),
        grid=grid,
        in_specs=[q_spec] + kv_specs,
        out_specs=out_spec,
        scratch_shapes=[pltpu.VMEM((2, tq, keys), F32), pltpu.VMEM((2, tq, keys), BF16)],
        compiler_params=_params("arbitrary", "arbitrary"),
        name="mla_lat" if latent else "mla_ctx",
    )(*args)


def _outmlp_kernel(*refs, n_x, with_ctx, n_lat_blocks):
    mod_ref = refs[n_x]
    n_y = 6 if with_ctx else 3
    y_refs = refs[n_x + 1:n_x + 1 + n_y]
    gpost_ref, gpre_ref, gmlp_ref, wo_ref, w1_ref, w2_ref, o_ref = refs[n_x + 1 + n_y:]
    if with_ctx:
        is_lat = pl.program_id(0) < n_lat_blocks
        parts = [jnp.where(is_lat, y_refs[2 * i][...], y_refs[2 * i + 1][...]) for i in range(3)]
    else:
        parts = [r[...] for r in y_refs]
    y = jnp.concatenate(parts, axis=1)
    t = jnp.dot(y, wo_ref[...], preferred_element_type=F32)
    x1 = _read_token_rows(refs[:n_x], n_lat_blocks) + mod_ref[2:3, :] * _rms(t, gpost_ref[...])
    h2 = (_rms(x1, gpre_ref[...]) * (1.0 + mod_ref[4:5, :]) + mod_ref[3:4, :]).astype(BF16)
    acc = jnp.zeros(x1.shape, F32)
    for f in range(D_FF // FF_CHUNK):
        cols = slice(f * FF_CHUNK, (f + 1) * FF_CHUNK)
        a = jnp.dot(h2, w1_ref[:, cols], preferred_element_type=F32)
        a = jnp.square(jnp.maximum(a, 0.0)).astype(BF16)
        acc = acc + jnp.dot(a, w2_ref[cols, :], preferred_element_type=F32)
    o_ref[...] = x1 + mod_ref[5:6, :] * _rms(acc, gmlp_ref[...])


def _outmlp_call(xs, mods_l, ys, gpost, gpre, gmlp, wo, w1, w2, *, batch, seq, with_ctx):
    n = sum(a.shape[0] for a in xs) if isinstance(xs, tuple) else xs.shape[0]
    nlb = batch * seq // TM
    nblk = n // TM if with_ctx else nlb
    per_b = seq // TM
    x_specs, x_args = _token_rows(xs, nlb, D_MODEL)
    modrow = lambda i: jnp.where(i < nlb, i // per_b, batch)
    rowblk = lambda w: pl.BlockSpec((TM, w), lambda i: (i, 0))
    resident = lambda a: pl.BlockSpec(a.shape, lambda i: (0,) * a.ndim, pipeline_mode=pl.Buffered(1))
    y_specs, y_args = [], []
    for y_lat, y_ctx in ys:
        w = y_lat.shape[1]
        y_specs.append(pl.BlockSpec((TM, w), lambda i: (jnp.minimum(i, nlb - 1), 0)))
        y_args.append(y_lat)
        if with_ctx:
            y_specs.append(pl.BlockSpec((TM, w), lambda i: (jnp.maximum(i - nlb, 0), 0)))
            y_args.append(y_ctx)
    return pl.pallas_call(
        functools.partial(_outmlp_kernel, n_x=len(x_args), with_ctx=with_ctx, n_lat_blocks=nlb),
        out_shape=jax.ShapeDtypeStruct((nblk * TM, D_MODEL), F32),
        grid=(nblk,),
        in_specs=x_specs + [pl.BlockSpec((None, 6, D_MODEL), lambda i: (modrow(i), 0, 0))] + y_specs
                 + [resident(gpost), resident(gpre), resident(gmlp), resident(wo), resident(w1), resident(w2)],
        out_specs=rowblk(D_MODEL),
        compiler_params=_params("arbitrary"),
        name="outproj_mlp",
    )(*x_args, mods_l, *y_args, gpost, gpre, gmlp, wo, w1, w2)


def _rope_rot_cols(w):
    w4 = w.reshape(w.shape[:-1] + (2, 2, MLA_ROPE // 4))
    return jnp.stack([-w4[..., 1, :], w4[..., 0, :]], axis=-2).reshape(w.shape)


def _zeros_like_cols(w, n):
    return jnp.zeros(w.shape[:-1] + (n,), w.dtype)


def _layout_w_in(w_in):
    o = np.concatenate([[0], np.cumsum(IN_SIZES)])
    qkv, z, beta, dec, lx, lg, cq, ckv, kr = [w_in[..., o[i]:o[i + 1]] for i in range(len(IN_SIZES))]
    zc = functools.partial(_zeros_like_cols, w_in)
    main = jnp.concatenate([qkv, z, lx, lg], axis=-1)
    pad = LANES - MLA_NOPE - MLA_ROPE
    rest = jnp.concatenate([cq, ckv, beta, dec, zc(LANES - 4 * GDN_HEADS),
                            zc(MLA_NOPE), kr, zc(pad), zc(MLA_NOPE), _rope_rot_cols(kr), zc(pad)], axis=-1)
    return main.astype(BF16), rest.astype(BF16)


def _layout_mla(w_uq, w_ukv):
    lead = w_uq.shape[:-1]
    q4 = w_uq.reshape(lead + (MLA_HEADS, MLA_NOPE + MLA_ROPE))
    nope, rope = q4[..., :MLA_NOPE], q4[..., MLA_NOPE:]
    pad = LANES - MLA_NOPE - MLA_ROPE
    wa = jnp.concatenate([nope, rope, _zeros_like_cols(rope, pad)], axis=-1)
    wb = jnp.concatenate([jnp.zeros_like(nope), _rope_rot_cols(rope), _zeros_like_cols(rope, pad)], axis=-1)
    wq = jnp.concatenate([wa.reshape(lead + (-1,)), wb.reshape(lead + (-1,))], axis=-1)
    lead = w_ukv.shape[:-1]
    kv4 = w_ukv.reshape(lead + (MLA_HEADS, MLA_NOPE + MLA_V))
    k_nope, v = kv4[..., :MLA_NOPE], kv4[..., MLA_NOPE:]
    wkn = jnp.concatenate([k_nope, _zeros_like_cols(k_nope, LANES - MLA_NOPE)], axis=-1).reshape(lead + (-1,))
    return wq.astype(BF16), wkn.astype(BF16), v.reshape(lead + (-1,)).astype(BF16)


def _rope_table(seq):
    rows = seq // GRID_W
    row = jnp.repeat(jnp.arange(rows, dtype=F32), GRID_W)
    col = jnp.tile(jnp.arange(GRID_W, dtype=F32), rows)
    half = MLA_ROPE // 2
    inv = ROPE_BASE ** (-jnp.arange(0, half, 2, dtype=F32) / half)
    ang = jnp.stack([row[:, None] * inv, col[:, None] * inv], axis=1)
    ang = jnp.concatenate([ang, ang], axis=-1).reshape(seq, MLA_ROPE)
    cos = jnp.concatenate([jnp.cos(ang), jnp.ones((TM, MLA_ROPE), F32)], axis=0)
    sin = jnp.concatenate([jnp.sin(ang), jnp.zeros((TM, MLA_ROPE), F32)], axis=0)
    n = seq + TM
    one, zero = jnp.ones((n, MLA_NOPE), F32), jnp.zeros((n, MLA_NOPE), F32)
    pad = jnp.zeros((n, LANES - MLA_NOPE - MLA_ROPE), F32)
    scale = (MLA_NOPE + MLA_ROPE) ** -0.5 * math.log2(math.e)
    return jnp.concatenate([scale * one, scale * cos, pad, zero, scale * sin, pad,
                            zero, cos, pad, zero, sin, pad], axis=1)


def _layout_lru(w_a, b_a, w_i, b_i):
    def blockdiag(w):
        g, n = w.shape[1], w.shape[2]
        on_diag = jnp.eye(g, dtype=bool)[None, :, None, :, None]
        return jnp.where(on_diag, w[:, :, :, None, :], 0.0).reshape(w.shape[0], g * n, g * n)

    da, di = blockdiag(w_a), blockdiag(w_i)
    w = jnp.concatenate([da[0], di[0], da[1], di[1]], axis=-1).astype(BF16)
    b = jnp.concatenate([b_a[0], b_i[0], b_a[1], b_i[1]], axis=-1)[None, :]
    return w, b


def kernel(x, c, ctx, c_ctx, w_ada, b_ada, g_attn_pre, g_attn_post, g_mlp_pre, g_mlp_post, w_in, gdn_conv_w,
           gdn_a_log, gdn_dt_bias, gdn_norm_w, lru_conv_w, lru_conv_b, lru_w_a, lru_b_a, lru_w_i, lru_b_i,
           lru_lambda, mla_q_norm, mla_w_uq, mla_kv_norm, mla_w_ukv, w_out, w_mlp1, w_mlp2):
    batch, seq, _ = x.shape
    n_ctx = ctx.shape[1]
    depth = w_ada.shape[0]
    assert seq % TM == 0 and (batch * n_ctx) % TM == 0 and n_ctx % CONV_ROWS == 0 and seq % GRID_W == 0
    assert (batch * seq) % n_ctx == 0 and batch + 1 <= 2 * SUBLANES
    dims = dict(batch=batch, seq=seq, ctx=n_ctx)

    cond = jnp.concatenate([c, c_ctx[None, :], jnp.zeros((2 * SUBLANES - batch - 1, D_MODEL), F32)], axis=0)
    mods = _ada_call(cond, w_ada, b_ada).reshape(depth, 2 * SUBLANES, 6, D_MODEL)
    tab = _rope_table(seq)
    xs = (x.reshape(batch * seq, D_MODEL), ctx.reshape(batch * n_ctx, D_MODEL))
    row = lambda a: a[None, :]

    for l in range(depth):
        last = l == depth - 1
        wm, wr = _layout_w_in(w_in[l])
        wq, wkn, wv = _layout_mla(mla_w_uq[l], mla_w_ukv[l])
        pad = jnp.zeros((LANES - 4 * GDN_HEADS,), F32)
        ald = jnp.stack([jnp.concatenate([jnp.zeros((2 * GDN_HEADS,), F32), gdn_a_log[l].reshape(-1), pad]),
                         jnp.concatenate([jnp.zeros((2 * GDN_HEADS,), F32), gdn_dt_bias[l].reshape(-1), pad])])
        p, gg, q, kt, v = _inproj_call(xs, mods[l], row(g_attn_pre[l]), wm, wr, row(mla_q_norm[l]), wq,
                                      row(mla_kv_norm[l]), wkn, wv, tab, ald, batch=batch, seq=seq)
        y_gdn = _gdn_call(p, gg, gdn_conv_w[l], row(gdn_norm_w[l]), **dims)
        w_blk, b_blk = _layout_lru(lru_w_a[l], lru_b_a[l], lru_w_i[l], lru_b_i[l])
        y_lru = _lru_call(p, lru_conv_w[l], row(lru_conv_b[l]), w_blk, b_blk, lru_lambda[l], **dims)
        y_att = (_attn_call(q, kt, v, latent=True, **dims),
                 None if last else _attn_call(q, kt, v, latent=False, **dims))
        xs = _outmlp_call(xs, mods[l], (y_gdn, y_lru, y_att), row(g_attn_post[l]), row(g_mlp_pre[l]),
                          row(g_mlp_post[l]), w_out[l].astype(BF16), w_mlp1[l].astype(BF16),
                          w_mlp2[l].astype(BF16), batch=batch, seq=seq, with_ctx=not last)
    return xs.reshape(batch, seq, D_MODEL)
```
